```python
import jax, jax.numpy as jnp
from jax import lax
import numpy as np

D_MODEL = 2048
BATCH = 4
SEQ = 4096
DEPTH = 2

D_MIX = D_MODEL
WIDTH_A = D_MIX // 2
WIDTH_B = D_MIX - WIDTH_A
CHUNK = 128
HEADS_A = 8
DH_A = WIDTH_A // HEADS_A
HEADS_B = 16
DH_B = WIDTH_B // HEADS_B
Q_BLOCK = 128
EPS = 1e-6
SPLIT_SIZES = (WIDTH_A, WIDTH_A, WIDTH_A, WIDTH_B, WIDTH_B, WIDTH_B, WIDTH_B, HEADS_B)
D_IN = sum(SPLIT_SIZES)

kernel_name = "hybrid_gmlp_forgetting_attn_block"


def _split_points(sizes):
    pts, acc = [], 0
    for s in sizes[:-1]:
        acc += s
        pts.append(acc)
    return pts


def rms_norm(x, g):
    xf = x.astype(jnp.float32)
    y = xf * lax.rsqrt(jnp.mean(xf * xf, axis=-1, keepdims=True) + EPS)
    return (y * g.astype(jnp.float32)).astype(x.dtype)


def layer_norm(x, g, b):
    xf = x.astype(jnp.float32)
    mu = jnp.mean(xf, axis=-1, keepdims=True)
    xc = xf - mu
    var = jnp.mean(xc * xc, axis=-1, keepdims=True)
    y = xc * lax.rsqrt(var + EPS) * g.astype(jnp.float32) + b.astype(jnp.float32)
    return y.astype(x.dtype)


def chunked_spatial_gating(u, v, ln_g, ln_b, w_s, b_s):
    bsz, s, _ = v.shape
    n = s // CHUNK
    v = layer_norm(v, ln_g, ln_b)
    vc = v.reshape(bsz, n, CHUNK, HEADS_A, DH_A)
    causal = jnp.tril(jnp.ones((CHUNK, CHUNK), dtype=bool))
    w = jnp.where(causal[None], w_s, jnp.zeros_like(w_s))
    mixed = jnp.einsum('hts,bnshd->bnthd', w, vc) + b_s.T[None, None, :, :, None]
    return u * mixed.reshape(bsz, s, WIDTH_A)


def forgetting_attention(q, k, v, f_logit, b_f):
    bsz, s, _ = q.shape
    nb = s // Q_BLOCK
    to_heads = lambda t: t.reshape(bsz, s, HEADS_B, DH_B).transpose(0, 2, 1, 3)
    q, k, v = to_heads(q), to_heads(k), to_heads(v)
    log_f = jax.nn.log_sigmoid(f_logit.astype(jnp.float32) + b_f.astype(jnp.float32))
    F = jnp.cumsum(log_f, axis=1).transpose(0, 2, 1)
    qb = q.reshape(bsz, HEADS_B, nb, Q_BLOCK, DH_B).transpose(2, 0, 1, 3, 4)
    Fb = F.reshape(bsz, HEADS_B, nb, Q_BLOCK).transpose(2, 0, 1, 3)
    k_pos = jnp.arange(s)
    scale = DH_B ** -0.5

    def block(args):
        qi, Fi, i = args
        q_pos = i * Q_BLOCK + jnp.arange(Q_BLOCK)
        logits = jnp.einsum('bhqd,bhkd->bhqk', qi, k).astype(jnp.float32) * scale
        logits = logits + (Fi[..., :, None] - F[..., None, :])
        logits = jnp.where(q_pos[:, None] >= k_pos[None, :], logits, -jnp.inf)
        p = jax.nn.softmax(logits, axis=-1)
        return jnp.einsum('bhqk,bhkd->bhqd', p.astype(v.dtype), v)

    out = lax.map(block, (qb, Fb, jnp.arange(nb)))
    return out.transpose(1, 0, 3, 2, 4).reshape(bsz, s, WIDTH_B)


def setup_inputs(seed: int = 0) -> dict:
    key = jax.random.key(seed)
    ks = jax.random.split(key, 16)
    f32 = jnp.float32
    nrm = lambda k, shape: jax.random.normal(k, shape, dtype=f32)
    row_scale = (1.0 / jnp.sqrt(jnp.arange(1, CHUNK + 1, dtype=f32)))[None, None, :, None]
    b_f = jnp.linspace(1.0, 5.0, HEADS_B, dtype=f32)[None, :] + 0.1 * nrm(ks[11], (DEPTH, HEADS_B))
    return {
        "x": nrm(ks[0], (BATCH, SEQ, D_MODEL)),
        "c": nrm(ks[1], (BATCH, D_MODEL)),
        "norm_g": 1.0 + 0.1 * nrm(ks[2], (DEPTH, D_MODEL)),
        "w_ada": nrm(ks[3], (DEPTH, D_MODEL, 3 * D_MODEL)) * D_MODEL ** -0.5,
        "b_ada": 0.01 * nrm(ks[4], (DEPTH, 3 * D_MODEL)),
        "w_in": nrm(ks[5], (DEPTH, D_MODEL, D_IN)) * D_MODEL ** -0.5,
        "ln_v_g": 1.0 + 0.1 * nrm(ks[6], (DEPTH, WIDTH_A)),
        "ln_v_b": 0.01 * nrm(ks[7], (DEPTH, WIDTH_A)),
        "w_s": nrm(ks[8], (DEPTH, HEADS_A, CHUNK, CHUNK)) * row_scale,
        "b_s": 1.0 + 0.1 * nrm(ks[9], (DEPTH, HEADS_A, CHUNK)),
        "b_f": b_f,
        "w_out": nrm(ks[10], (DEPTH, D_MIX, D_MODEL)) * D_MIX ** -0.5,
        "final_g": 1.0 + 0.1 * nrm(ks[12], (D_MODEL,)),
    }


def reference(x, c, norm_g, w_ada, b_ada, w_in, ln_v_g, ln_v_b, w_s, b_s, b_f, w_out, final_g):
    cond = jax.nn.silu(c)
    pts = _split_points(SPLIT_SIZES)
    for l in range(DEPTH):
        mod = cond @ w_ada[l] + b_ada[l]
        shift, scale, gate = jnp.split(mod, 3, axis=-1)
        h = rms_norm(x, norm_g[l]) * (1.0 + scale[:, None, :]) + shift[:, None, :]
        z = h @ w_in[l]
        u_a, v_a, g_a, q_b, k_b, v_b, g_b, f_b = jnp.split(z, pts, axis=-1)
        y_a = chunked_spatial_gating(jax.nn.gelu(u_a, approximate=False),
                                     jax.nn.gelu(v_a, approximate=False),
                                     ln_v_g[l], ln_v_b[l], w_s[l], b_s[l]) * jax.nn.silu(g_a)
        y_b = forgetting_attention(q_b, k_b, v_b, f_b, b_f[l]) * jax.nn.silu(g_b)
        y = jnp.concatenate([y_a, y_b], axis=-1) @ w_out[l]
        x = x + gate[:, None, :] * y
    return rms_norm(x, final_g)
```

```python
import functools

import jax
import jax.numpy as jnp
import numpy as np
from jax import lax
from jax.experimental import pallas as pl
from jax.experimental.pallas import tpu as pltpu

D_MODEL = 2048
WIDTH_A = 1024
WIDTH_B = 1024
CHUNK = 128
HEADS_A = 8
DH_A = WIDTH_A // HEADS_A
HEADS_B = 16
DH_B = WIDTH_B // HEADS_B
EPS = 1e-6

LANES = 128
HEAD_PAIRS = HEADS_B // 2
AUG_Q = 2 * LANES
AUG_K = LANES
N_MAIN = 3 * WIDTH_A + 4 * WIDTH_B
VMEM_LIMIT = 56 * 1024 * 1024

COL_U, COL_V, COL_GA, COL_Q, COL_K, COL_VB, COL_GB = range(7)

BF16 = jnp.bfloat16
F32 = jnp.float32


def _silu(x):
    return x / (1.0 + jnp.exp(-x))


def _gelu(x):
    return 0.5 * x * (1.0 + lax.erf(x * np.float32(1.0 / np.sqrt(2.0))))


def _ada_kernel(c_ref, w_ref, b_ref, o_ref):
    cond = _silu(c_ref[...]).astype(BF16)
    o_ref[...] = jnp.dot(cond, w_ref[...].astype(BF16), preferred_element_type=F32) + b_ref[...]


def _ada_modulation(c_pad, w_ada, b_ada):
    depth, d, n = w_ada.shape
    tn = 1536
    rows = c_pad.shape[0]
    return pl.pallas_call(
        _ada_kernel,
        grid=(depth, n // tn),
        in_specs=[
            pl.BlockSpec((rows, d), lambda l, j: (0, 0)),
            pl.BlockSpec((None, d, tn), lambda l, j: (l, 0, j)),
            pl.BlockSpec((None, 1, tn), lambda l, j: (l, 0, j)),
        ],
        out_specs=pl.BlockSpec((None, rows, tn), lambda l, j: (l, 0, j)),
        out_shape=jax.ShapeDtypeStruct((depth, rows, n), F32),
        name="ada_modulation",
        compiler_params=pltpu.CompilerParams(
            dimension_semantics=("arbitrary", "arbitrary"), vmem_limit_bytes=VMEM_LIMIT),
    )(c_pad, w_ada, b_ada.reshape(depth, 1, n))


def _inproj_kernel(x_ref, shift_ref, scale_ref, g_ref, w_ref, wf_ref, z_ref, f_ref, h_ref):
    j = pl.program_id(1)

    @pl.when(j == 0)
    def _():
        x = x_ref[...]
        y = x * lax.rsqrt(jnp.mean(x * x, axis=-1, keepdims=True) + EPS) * g_ref[...]
        h = (y * (1.0 + scale_ref[...]) + shift_ref[...]).astype(BF16)
        h_ref[...] = h
        f_ref[...] = jnp.dot(h, wf_ref[...], preferred_element_type=F32)

    acc = jnp.dot(h_ref[...], w_ref[...], preferred_element_type=F32)

    @pl.when(j <= COL_V)
    def _():
        z_ref[...] = _gelu(acc).astype(BF16)

    @pl.when((j == COL_GA) | (j == COL_GB))
    def _():
        z_ref[...] = _silu(acc).astype(BF16)

    @pl.when(j == COL_Q)
    def _():
        z_ref[...] = (acc * DH_B ** -0.5).astype(BF16)

    @pl.when((j == COL_K) | (j == COL_VB))
    def _():
        z_ref[...] = acc.astype(BF16)


def _inproj(x2, shift, scale, g, w_main, w_f, seq):
    m, d = x2.shape
    tm, tn = 1024, 1024
    per_batch = seq // tm
    return pl.pallas_call(
        _inproj_kernel,
        grid=(m // tm, N_MAIN // tn),
        in_specs=[
            pl.BlockSpec((tm, d), lambda i, j: (i, 0)),
            pl.BlockSpec((None, 1, d), lambda i, j: (i // per_batch, 0, 0)),
            pl.BlockSpec((None, 1, d), lambda i, j: (i // per_batch, 0, 0)),
            pl.BlockSpec((1, d), lambda i, j: (0, 0)),
            pl.BlockSpec((d, tn), lambda i, j: (0, j)),
            pl.BlockSpec((d, LANES), lambda i, j: (0, 0)),
        ],
        out_specs=[
            pl.BlockSpec((tm, tn), lambda i, j: (i, j)),
            pl.BlockSpec((tm, LANES), lambda i, j: (i, 0)),
        ],
        out_shape=[
            jax.ShapeDtypeStruct((m, N_MAIN), BF16),
            jax.ShapeDtypeStruct((m, LANES), F32),
        ],
        scratch_shapes=[pltpu.VMEM((tm, d), BF16)],
        name="norm_inproj",
        compiler_params=pltpu.CompilerParams(
            dimension_semantics=("parallel", "arbitrary"), vmem_limit_bytes=VMEM_LIMIT),
    )(x2, shift, scale, g, w_main, w_f)


def _split3(x):
    hi = x.astype(BF16)
    r1 = x - hi.astype(F32)
    mid = r1.astype(BF16)
    lo = (r1 - mid.astype(F32)).astype(BF16)
    return hi, mid, lo


def _forget_kernel(f_ref, bf_ref, selq_ref, selk_ref, cq_ref, ck_ref, qa_ref, ka_ref, carry_ref):
    t = pl.program_id(1)

    @pl.when(t == 0)
    def _():
        carry_ref[...] = jnp.zeros_like(carry_ref)

    x = f_ref[...] + bf_ref[...]
    log_f = -(jnp.maximum(-x, 0.0) + jnp.log1p(jnp.exp(-jnp.abs(x))))
    ts = x.shape[0]
    row = lax.broadcasted_iota(jnp.int32, (ts, ts), 0)
    col = lax.broadcasted_iota(jnp.int32, (ts, ts), 1)
    tri = (row >= col).astype(F32)
    cum = jnp.dot(tri, log_f, preferred_element_type=F32,
                  precision=lax.Precision.HIGHEST) + carry_ref[...]
    carry_ref[...] = cum[ts - 1:ts, :]
    pieces = jnp.concatenate(_split3(cum), axis=1)
    qa_ref[...] = (jnp.dot(pieces, selq_ref[...], preferred_element_type=F32)
                   + cq_ref[...]).astype(BF16)
    ka_ref[...] = (jnp.dot(pieces, selk_ref[...], preferred_element_type=F32)
                   + ck_ref[...]).astype(BF16)


def _bias_selectors():
    selq = np.zeros((3 * LANES, HEAD_PAIRS * AUG_Q), np.float32)
    selk = np.zeros((3 * LANES, HEAD_PAIRS * AUG_K), np.float32)
    cq = np.zeros((1, HEAD_PAIRS * AUG_Q), np.float32)
    ck = np.zeros((1, HEAD_PAIRS * AUG_K), np.float32)
    for p in range(HEAD_PAIRS):
        for which in range(2):
            head = 2 * p + which
            qbase = p * AUG_Q + which * LANES + 6 * which
            kbase = p * AUG_K + 6 * which
            for piece in range(3):
                selq[piece * LANES + head, qbase + piece] = 1.0
                cq[0, qbase + 3 + piece] = 1.0
                ck[0, kbase + piece] = 1.0
                selk[piece * LANES + head, kbase + 3 + piece] = -1.0
    return (jnp.asarray(selq, BF16), jnp.asarray(selk, BF16), jnp.asarray(cq), jnp.asarray(ck))


def _forget_bias(f, b_f_pad, seq):
    m = f.shape[0]
    ts = 512
    per_batch = seq // ts
    selq, selk, cq, ck = _bias_selectors()
    const = lambda shape: pl.BlockSpec(shape, lambda b, t: (0, 0))
    return pl.pallas_call(
        _forget_kernel,
        grid=(m // seq, per_batch),
        in_specs=[
            pl.BlockSpec((ts, LANES), lambda b, t: (b * per_batch + t, 0)),
            const((1, LANES)),
            const(selq.shape), const(selk.shape), const(cq.shape), const(ck.shape),
        ],
        out_specs=[
            pl.BlockSpec((ts, HEAD_PAIRS * AUG_Q), lambda b, t: (b * per_batch + t, 0)),
            pl.BlockSpec((ts, HEAD_PAIRS * AUG_K), lambda b, t: (b * per_batch + t, 0)),
        ],
        out_shape=[
            jax.ShapeDtypeStruct((m, HEAD_PAIRS * AUG_Q), BF16),
            jax.ShapeDtypeStruct((m, HEAD_PAIRS * AUG_K), BF16),
        ],
        scratch_shapes=[pltpu.VMEM((1, LANES), F32)],
        name="forget_bias",
        compiler_params=pltpu.CompilerParams(
            dimension_semantics=("parallel", "arbitrary"), vmem_limit_bytes=VMEM_LIMIT),
    )(f, b_f_pad, selq, selk, cq, ck)


def _sgu_kernel(u_ref, v_ref, ga_ref, lng_ref, lnb_ref, ws_ref, bst_ref, o_ref):
    rows = u_ref.shape[0]
    v = v_ref[...].astype(F32)
    mu = jnp.mean(v, axis=-1, keepdims=True)
    vc = v - mu
    var = jnp.mean(vc * vc, axis=-1, keepdims=True)
    vn = (vc * lax.rsqrt(var + EPS) * lng_ref[...] + lnb_ref[...]).astype(BF16)
    trow = lax.broadcasted_iota(jnp.int32, (CHUNK, CHUNK), 0)
    tcol = lax.broadcasted_iota(jnp.int32, (CHUNK, CHUNK), 1)
    causal = trow >= tcol
    for h in range(HEADS_A):
        w = jnp.where(causal, ws_ref[h], 0.0).astype(BF16)
        bias = bst_ref[:, h:h + 1]
        cols = slice(h * DH_A, (h + 1) * DH_A)
        for c in range(rows // CHUNK):
            rws = slice(c * CHUNK, (c + 1) * CHUNK)
            mixed = jnp.dot(w, vn[rws, cols], preferred_element_type=F32) + bias
            y = u_ref[rws, cols].astype(F32) * mixed * ga_ref[rws, cols].astype(F32)
            o_ref[rws, cols] = y.astype(BF16)


def _spatial_gating(z, ln_g, ln_b, w_s, b_s_t):
    m = z.shape[0]
    tm = 512
    zcol = lambda col: pl.BlockSpec((tm, WIDTH_A), lambda i: (i, col))
    const2 = lambda shape: pl.BlockSpec(shape, lambda i: (0, 0))
    return pl.pallas_call(
        _sgu_kernel,
        grid=(m // tm,),
        in_specs=[
            zcol(COL_U), zcol(COL_V), zcol(COL_GA),
            const2((1, WIDTH_A)), const2((1, WIDTH_A)),
            pl.BlockSpec((HEADS_A, CHUNK, CHUNK), lambda i: (0, 0, 0)),
            const2((CHUNK, HEADS_A)),
        ],
        out_specs=pl.BlockSpec((tm, WIDTH_A), lambda i: (i, 0)),
        out_shape=jax.ShapeDtypeStruct((m, WIDTH_A), BF16),
        name="spatial_gating",
        compiler_params=pltpu.CompilerParams(
            dimension_semantics=("parallel",), vmem_limit_bytes=VMEM_LIMIT),
    )(z, z, z, ln_g, ln_b, w_s, b_s_t)


def _attn_kernel(q_ref, k_ref, v_ref, gb_ref, qa_ref, ka_ref, o_ref, *, tile):
    qi = pl.program_id(2)
    lane = lax.broadcasted_iota(jnp.int32, (tile, LANES), 1)
    first = lane < DH_B
    q2 = q_ref[...]
    zero = jnp.zeros_like(q2)
    lhs = (
        jnp.concatenate([jnp.where(first, q2, zero), qa_ref[:, :LANES]], axis=1),
        jnp.concatenate([jnp.where(first, zero, q2), qa_ref[:, LANES:]], axis=1),
    )
    nt = (((1,), (1,)), ((), ()))

    def step(j, carry, masked):
        rows = pl.ds(pl.multiple_of(j * tile, tile), tile)
        rhs = jnp.concatenate([k_ref[rows, :], ka_ref[rows, :]], axis=1)
        v2 = v_ref[rows, :]
        out = []
        for h in range(2):
            m, l, acc = carry[h]
            s = lax.dot_general(lhs[h], rhs, nt, preferred_element_type=F32)
            if masked:
                r = lax.broadcasted_iota(jnp.int32, (tile, tile), 0)
                c = lax.broadcasted_iota(jnp.int32, (tile, tile), 1)
                s = jnp.where(r >= c, s, -jnp.inf)
            m_new = jnp.maximum(m, jnp.max(s, axis=-1, keepdims=True))
            alpha = jnp.exp(m - m_new)
            p = jnp.exp(s - m_new)
            l = alpha * l + jnp.sum(p, axis=-1, keepdims=True)
            acc = alpha * acc + jnp.dot(p.astype(BF16), v2, preferred_element_type=F32)
            out.append((m_new, l, acc))
        return tuple(out)

    init = tuple((jnp.full((tile, 1), -jnp.inf, F32), jnp.zeros((tile, 1), F32),
                  jnp.zeros((tile, LANES), F32)) for _ in range(2))
    carry = lax.fori_loop(0, qi, lambda j, c: step(j, c, False), init)
    (_, l_a, acc_a), (_, l_b, acc_b) = step(qi, carry, True)
    o = jnp.where(first, acc_a / l_a, acc_b / l_b)
    o_ref[...] = (o * gb_ref[...].astype(F32)).astype(BF16)


def _attention(z, qaug, kaug, batch, seq):
    m = z.shape[0]
    tile = 256
    nq = seq // tile
    blk = WIDTH_B // LANES
    qrow = lambda b, p, i: b * nq + i
    return pl.pallas_call(
        functools.partial(_attn_kernel, tile=tile),
        grid=(batch, HEAD_PAIRS, nq),
        in_specs=[
            pl.BlockSpec((tile, LANES), lambda b, p, i: (qrow(b, p, i), COL_Q * blk + p)),
            pl.BlockSpec((seq, LANES), lambda b, p, i: (b, COL_K * blk + p)),
            pl.BlockSpec((seq, LANES), lambda b, p, i: (b, COL_VB * blk + p)),
            pl.BlockSpec((tile, LANES), lambda b, p, i: (qrow(b, p, i), COL_GB * blk + p)),
            pl.BlockSpec((tile, AUG_Q), lambda b, p, i: (qrow(b, p, i), p)),
            pl.BlockSpec((seq, AUG_K), lambda b, p, i: (b, p)),
        ],
        out_specs=pl.BlockSpec((tile, LANES), lambda b, p, i: (qrow(b, p, i), p)),
        out_shape=jax.ShapeDtypeStruct((m, WIDTH_B), BF16),
        name="fox_attention",
        compiler_params=pltpu.CompilerParams(
            dimension_semantics=("parallel", "parallel", "arbitrary"),
            vmem_limit_bytes=VMEM_LIMIT),
    )(z, z, z, z, qaug, kaug)


def _outproj_kernel(ya_ref, yb_ref, x_ref, gate_ref, wa_ref, wb_ref, fg_ref, o_ref, *, final):
    y = jnp.dot(ya_ref[...], wa_ref[...], preferred_element_type=F32)
    y = y + jnp.dot(yb_ref[...], wb_ref[...], preferred_element_type=F32)
    x = x_ref[...] + gate_ref[...] * y
    if final:
        x = x * lax.rsqrt(jnp.mean(x * x, axis=-1, keepdims=True) + EPS) * fg_ref[...]
    o_ref[...] = x


def _outproj(ya, yb, x2, gate, w_out, final_g, seq, final):
    m, d = x2.shape
    tm = 512
    per_batch = seq // tm
    return pl.pallas_call(
        functools.partial(_outproj_kernel, final=final),
        grid=(m // tm,),
        in_specs=[
            pl.BlockSpec((tm, WIDTH_A), lambda i: (i, 0)),
            pl.BlockSpec((tm, WIDTH_B), lambda i: (i, 0)),
            pl.BlockSpec((tm, d), lambda i: (i, 0)),
            pl.BlockSpec((None, 1, d), lambda i: (i // per_batch, 0, 0)),
            pl.BlockSpec((WIDTH_A, d), lambda i: (0, 0)),
            pl.BlockSpec((WIDTH_B, d), lambda i: (1, 0)),
            pl.BlockSpec((1, d), lambda i: (0, 0)),
        ],
        out_specs=pl.BlockSpec((tm, d), lambda i: (i, 0)),
        out_shape=jax.ShapeDtypeStruct((m, d), F32),
        name="outproj_final" if final else "outproj",
        compiler_params=pltpu.CompilerParams(
            dimension_semantics=("parallel",), vmem_limit_bytes=VMEM_LIMIT),
    )(ya, yb, x2, gate, w_out, w_out, final_g)


def kernel(x, c, norm_g, w_ada, b_ada, w_in, ln_v_g, ln_v_b, w_s, b_s, b_f, w_out, final_g):
    batch, seq, d = x.shape
    depth = w_in.shape[0]
    c_pad = jnp.pad(c, ((0, 8 - batch), (0, 0)))
    mod = _ada_modulation(c_pad, w_ada, b_ada)[:, :batch]
    mod = mod.reshape(depth, batch, 3, 1, d)
    w_in_bf = w_in.astype(BF16)
    w_out_bf = w_out.astype(BF16)
    x2 = x.reshape(batch * seq, d)
    for l in range(depth):
        shift, scale, gate = mod[l, :, 0], mod[l, :, 1], mod[l, :, 2]
        w_f = jnp.pad(w_in_bf[l, :, N_MAIN:], ((0, 0), (0, LANES - HEADS_B)))
        z, f = _inproj(x2, shift, scale, norm_g[l][None], w_in_bf[l], w_f, seq)
        b_f_pad = jnp.pad(b_f[l], (0, LANES - HEADS_B))[None]
        qaug, kaug = _forget_bias(f, b_f_pad, seq)
        ya = _spatial_gating(z, ln_v_g[l][None], ln_v_b[l][None], w_s[l], b_s[l].T)
        yb = _attention(z, qaug, kaug, batch, seq)
        x2 = _outproj(ya, yb, x2, gate, w_out_bf[l], final_g[None], seq, final=(l == depth - 1))
    return x2.reshape(batch, seq, d)
```

```python
import functools

import jax
import jax.numpy as jnp
import numpy as np
from jax import lax
from jax.experimental import pallas as pl
from jax.experimental.pallas import tpu as pltpu

D_MODEL = 2048
WIDTH_A = 1024
WIDTH_B = 1024
CHUNK = 128
HEADS_A = 8
DH_A = WIDTH_A // HEADS_A
HEADS_B = 16
DH_B = WIDTH_B // HEADS_B
EPS = 1e-6

LANES = 128
HEAD_PAIRS = HEADS_B // 2
AUG_Q = 2 * LANES
AUG_K = LANES
N_MAIN = 3 * WIDTH_A + 4 * WIDTH_B
VMEM_LIMIT = 56 * 1024 * 1024

KEY_TILE = 256
QUERY_TILE = 1024
ONES_ROWS = 16
ACC_ROWS = DH_B + ONES_ROWS

COL_U, COL_V, COL_GA, COL_Q, COL_K, COL_VB, COL_GB = range(7)

BF16 = jnp.bfloat16
F32 = jnp.float32


def _silu(x):
    return x / (1.0 + jnp.exp(-x))


def _gelu(x):
    return 0.5 * x * (1.0 + lax.erf(x * np.float32(1.0 / np.sqrt(2.0))))


def _ada_kernel(c_ref, w_ref, b_ref, o_ref):
    cond = _silu(c_ref[...]).astype(BF16)
    o_ref[...] = jnp.dot(cond, w_ref[...].astype(BF16), preferred_element_type=F32) + b_ref[...]


def _ada_modulation(c_pad, w_ada, b_ada):
    depth, d, n = w_ada.shape
    tn = 1536
    rows = c_pad.shape[0]
    return pl.pallas_call(
        _ada_kernel,
        grid=(depth, n // tn),
        in_specs=[
            pl.BlockSpec((rows, d), lambda l, j: (0, 0)),
            pl.BlockSpec((None, d, tn), lambda l, j: (l, 0, j)),
            pl.BlockSpec((None, 1, tn), lambda l, j: (l, 0, j)),
        ],
        out_specs=pl.BlockSpec((None, rows, tn), lambda l, j: (l, 0, j)),
        out_shape=jax.ShapeDtypeStruct((depth, rows, n), F32),
        name="ada_modulation",
        compiler_params=pltpu.CompilerParams(
            dimension_semantics=("arbitrary", "arbitrary"), vmem_limit_bytes=VMEM_LIMIT),
    )(c_pad, w_ada, b_ada.reshape(depth, 1, n))


def _inproj_kernel(x_ref, shift_ref, scale_ref, g_ref, w_ref, wf_ref, z_ref, f_ref, h_ref):
    j = pl.program_id(1)

    @pl.when(j == 0)
    def _():
        x = x_ref[...]
        y = x * lax.rsqrt(jnp.mean(x * x, axis=-1, keepdims=True) + EPS) * g_ref[...]
        h = (y * (1.0 + scale_ref[...]) + shift_ref[...]).astype(BF16)
        h_ref[...] = h
        f_ref[...] = jnp.dot(h, wf_ref[...], preferred_element_type=F32)

    acc = jnp.dot(h_ref[...], w_ref[...], preferred_element_type=F32)

    @pl.when(j <= COL_V)
    def _():
        z_ref[...] = _gelu(acc).astype(BF16)

    @pl.when((j == COL_GA) | (j == COL_GB))
    def _():
        z_ref[...] = _silu(acc).astype(BF16)

    @pl.when(j == COL_Q)
    def _():
        z_ref[...] = (acc * DH_B ** -0.5).astype(BF16)

    @pl.when((j == COL_K) | (j == COL_VB))
    def _():
        z_ref[...] = acc.astype(BF16)


def _inproj(x2, shift, scale, g, w_main, w_f, seq):
    m, d = x2.shape
    tm, tn = 1024, 1024
    per_batch = seq // tm
    return pl.pallas_call(
        _inproj_kernel,
        grid=(m // tm, N_MAIN // tn),
        in_specs=[
            pl.BlockSpec((tm, d), lambda i, j: (i, 0)),
            pl.BlockSpec((None, 1, d), lambda i, j: (i // per_batch, 0, 0)),
            pl.BlockSpec((None, 1, d), lambda i, j: (i // per_batch, 0, 0)),
            pl.BlockSpec((1, d), lambda i, j: (0, 0)),
            pl.BlockSpec((d, tn), lambda i, j: (0, j)),
            pl.BlockSpec((d, LANES), lambda i, j: (0, 0)),
        ],
        out_specs=[
            pl.BlockSpec((tm, tn), lambda i, j: (i, j)),
            pl.BlockSpec((tm, LANES), lambda i, j: (i, 0)),
        ],
        out_shape=[
            jax.ShapeDtypeStruct((m, N_MAIN), BF16),
            jax.ShapeDtypeStruct((m, LANES), F32),
        ],
        scratch_shapes=[pltpu.VMEM((tm, d), BF16)],
        name="norm_inproj",
        compiler_params=pltpu.CompilerParams(
            dimension_semantics=("parallel", "arbitrary"), vmem_limit_bytes=VMEM_LIMIT),
    )(x2, shift, scale, g, w_main, w_f)


def _split3(x):
    hi = x.astype(BF16)
    r1 = x - hi.astype(F32)
    mid = r1.astype(BF16)
    lo = (r1 - mid.astype(F32)).astype(BF16)
    return hi, mid, lo


def _forget_kernel(f_ref, bf_ref, selq_ref, selk_ref, cq_ref, ck_ref, qa_ref, ka_ref, carry_ref):
    t = pl.program_id(1)

    @pl.when(t == 0)
    def _():
        carry_ref[...] = jnp.zeros_like(carry_ref)

    x = f_ref[...] + bf_ref[...]
    log_f = -(jnp.maximum(-x, 0.0) + jnp.log1p(jnp.exp(-jnp.abs(x))))
    ts = x.shape[0]
    row = lax.broadcasted_iota(jnp.int32, (ts, ts), 0)
    col = lax.broadcasted_iota(jnp.int32, (ts, ts), 1)
    tri = (row >= col).astype(F32)
    cum = jnp.dot(tri, log_f, preferred_element_type=F32,
                  precision=lax.Precision.HIGHEST) + carry_ref[...]
    carry_ref[...] = cum[ts - 1:ts, :]
    pieces = jnp.concatenate(_split3(cum), axis=1)
    qa_ref[...] = (jnp.dot(pieces, selq_ref[...], preferred_element_type=F32)
                   + cq_ref[...]).astype(BF16)
    ka_ref[...] = (jnp.dot(pieces, selk_ref[...], preferred_element_type=F32)
                   + ck_ref[...]).astype(BF16)


def _bias_selectors():
    selq = np.zeros((3 * LANES, HEAD_PAIRS * AUG_Q), np.float32)
    selk = np.zeros((3 * LANES, HEAD_PAIRS * AUG_K), np.float32)
    cq = np.zeros((1, HEAD_PAIRS * AUG_Q), np.float32)
    ck = np.zeros((1, HEAD_PAIRS * AUG_K), np.float32)
    for p in range(HEAD_PAIRS):
        for which in range(2):
            head = 2 * p + which
            qbase = p * AUG_Q + which * LANES + 6 * which
            kbase = p * AUG_K + 6 * which
            for piece in range(3):
                selq[piece * LANES + head, qbase + piece] = 1.0
                cq[0, qbase + 3 + piece] = 1.0
                ck[0, kbase + piece] = 1.0
                selk[piece * LANES + head, kbase + 3 + piece] = -1.0
    return (jnp.asarray(selq, BF16), jnp.asarray(selk, BF16), jnp.asarray(cq), jnp.asarray(ck))


def _forget_bias(f, b_f_pad, seq):
    m = f.shape[0]
    ts = 512
    per_batch = seq // ts
    selq, selk, cq, ck = _bias_selectors()
    const = lambda shape: pl.BlockSpec(shape, lambda b, t: (0, 0))
    return pl.pallas_call(
        _forget_kernel,
        grid=(m // seq, per_batch),
        in_specs=[
            pl.BlockSpec((ts, LANES), lambda b, t: (b * per_batch + t, 0)),
            const((1, LANES)),
            const(selq.shape), const(selk.shape), const(cq.shape), const(ck.shape),
        ],
        out_specs=[
            pl.BlockSpec((ts, HEAD_PAIRS * AUG_Q), lambda b, t: (b * per_batch + t, 0)),
            pl.BlockSpec((ts, HEAD_PAIRS * AUG_K), lambda b, t: (b * per_batch + t, 0)),
        ],
        out_shape=[
            jax.ShapeDtypeStruct((m, HEAD_PAIRS * AUG_Q), BF16),
            jax.ShapeDtypeStruct((m, HEAD_PAIRS * AUG_K), BF16),
        ],
        scratch_shapes=[pltpu.VMEM((1, LANES), F32)],
        name="forget_bias",
        compiler_params=pltpu.CompilerParams(
            dimension_semantics=("parallel", "arbitrary"), vmem_limit_bytes=VMEM_LIMIT),
    )(f, b_f_pad, selq, selk, cq, ck)


def _sgu_kernel(u_ref, v_ref, ga_ref, lng_ref, lnb_ref, ws_ref, bst_ref, o_ref):
    rows = u_ref.shape[0]
    v = v_ref[...].astype(F32)
    mu = jnp.mean(v, axis=-1, keepdims=True)
    vc = v - mu
    var = jnp.mean(vc * vc, axis=-1, keepdims=True)
    vn = (vc * lax.rsqrt(var + EPS) * lng_ref[...] + lnb_ref[...]).astype(BF16)
    trow = lax.broadcasted_iota(jnp.int32, (CHUNK, CHUNK), 0)
    tcol = lax.broadcasted_iota(jnp.int32, (CHUNK, CHUNK), 1)
    causal = trow >= tcol
    for h in range(HEADS_A):
        w = jnp.where(causal, ws_ref[h], 0.0).astype(BF16)
        bias = bst_ref[:, h:h + 1]
        cols = slice(h * DH_A, (h + 1) * DH_A)
        for c in range(rows // CHUNK):
            rws = slice(c * CHUNK, (c + 1) * CHUNK)
            mixed = jnp.dot(w, vn[rws, cols], preferred_element_type=F32) + bias
            y = u_ref[rws, cols].astype(F32) * mixed * ga_ref[rws, cols].astype(F32)
            o_ref[rws, cols] = y.astype(BF16)


def _spatial_gating(z, ln_g, ln_b, w_s, b_s_t):
    m = z.shape[0]
    tm = 512
    zcol = lambda col: pl.BlockSpec((tm, WIDTH_A), lambda i: (i, col))
    const2 = lambda shape: pl.BlockSpec(shape, lambda i: (0, 0))
    return pl.pallas_call(
        _sgu_kernel,
        grid=(m // tm,),
        in_specs=[
            zcol(COL_U), zcol(COL_V), zcol(COL_GA),
            const2((1, WIDTH_A)), const2((1, WIDTH_A)),
            pl.BlockSpec((HEADS_A, CHUNK, CHUNK), lambda i: (0, 0, 0)),
            const2((CHUNK, HEADS_A)),
        ],
        out_specs=pl.BlockSpec((tm, WIDTH_A), lambda i: (i, 0)),
        out_shape=jax.ShapeDtypeStruct((m, WIDTH_A), BF16),
        name="spatial_gating",
        compiler_params=pltpu.CompilerParams(
            dimension_semantics=("parallel",), vmem_limit_bytes=VMEM_LIMIT),
    )(z, z, z, ln_g, ln_b, w_s, b_s_t)


def _attn_kernel(q_ref, k_ref, v_ref, gb_ref, qa_ref, ka_ref, o_ref,
                 kk_ref, vt_ref, qt_ref, m_ref, acc_ref, s0_ref, s1_ref):
    qi = pl.program_id(2)
    tq = q_ref.shape[0]
    n_key_tiles = k_ref.shape[0] // KEY_TILE
    diag_tiles = tq // KEY_TILE

    @pl.when(qi == 0)
    def _():
        kk_ref[:, :LANES] = k_ref[...]
        kk_ref[:, LANES:] = ka_ref[...]
        ones = jnp.ones((ONES_ROWS, KEY_TILE), BF16)
        for t in range(n_key_tiles):
            vt = v_ref[t * KEY_TILE:(t + 1) * KEY_TILE, :].T
            vt_ref[0, t] = jnp.concatenate([vt[:DH_B], ones], axis=0)
            vt_ref[1, t] = jnp.concatenate([vt[DH_B:], ones], axis=0)

    q_t = q_ref[...].T
    qa_t = qa_ref[...].T
    zeros = jnp.zeros((DH_B, tq), BF16)
    qt_ref[0] = jnp.concatenate([q_t[:DH_B], zeros, qa_t[:LANES]], axis=0)
    qt_ref[1] = jnp.concatenate([zeros, q_t[DH_B:], qa_t[LANES:]], axis=0)
    m_ref[...] = jnp.full(m_ref.shape, -jnp.inf, F32)
    acc_ref[...] = jnp.zeros(acc_ref.shape, F32)

    def scores(t, s_ref, col0):
        kk = kk_ref[pl.ds(pl.multiple_of(t * KEY_TILE, KEY_TILE), KEY_TILE), :]
        for h in range(2):
            s_ref[h, :, col0:] = jnp.dot(kk, qt_ref[h, :, col0:], preferred_element_type=F32)

    def softmax_pv(t, s_ref, col0, masked):
        for h in range(2):
            s_t = s_ref[h, :, col0:]
            if masked:
                key = lax.broadcasted_iota(jnp.int32, s_t.shape, 0)
                qry = lax.broadcasted_iota(jnp.int32, s_t.shape, 1)
                s_t = jnp.where(key <= qry, s_t, -jnp.inf)
            m_old = m_ref[h, :, col0:]
            m_new = jnp.maximum(m_old, jnp.max(s_t, axis=0, keepdims=True))
            alpha = jnp.exp(m_old - m_new)
            p_t = jnp.exp(s_t - m_new).astype(BF16)
            pv = jnp.dot(vt_ref[h, t], p_t, preferred_element_type=F32)
            acc_ref[h, :, col0:] = alpha * acc_ref[h, :, col0:] + pv
            m_ref[h, :, col0:] = m_new

    s_bufs = (s0_ref, s1_ref)
    scores(0, s0_ref, 0)

    def body(u, carry):
        t = 2 * u
        scores(t + 1, s1_ref, 0)
        softmax_pv(t, s0_ref, 0, False)
        scores(t + 2, s0_ref, 0)
        softmax_pv(t + 1, s1_ref, 0, False)
        return carry

    first_diag = qi * diag_tiles
    lax.fori_loop(0, first_diag // 2, body, 0)
    for d in range(diag_tiles):
        if d + 1 < diag_tiles:
            scores(first_diag + d + 1, s_bufs[(d + 1) % 2], (d + 1) * KEY_TILE)
        softmax_pv(first_diag + d, s_bufs[d % 2], d * KEY_TILE, True)

    o_t = jnp.concatenate(
        [acc_ref[h, :DH_B, :] / acc_ref[h, DH_B:DH_B + 1, :] for h in range(2)], axis=0)
    o_ref[...] = (o_t.T * gb_ref[...].astype(F32)).astype(BF16)


def _attention(z, qaug, kaug, batch, seq):
    m = z.shape[0]
    tq = QUERY_TILE
    nq = seq // tq
    blk = WIDTH_B // LANES
    qrow = lambda b, p, i: b * nq + i
    return pl.pallas_call(
        _attn_kernel,
        grid=(batch, HEAD_PAIRS, nq),
        in_specs=[
            pl.BlockSpec((tq, LANES), lambda b, p, i: (qrow(b, p, i), COL_Q * blk + p)),
            pl.BlockSpec((seq, LANES), lambda b, p, i: (b, COL_K * blk + p)),
            pl.BlockSpec((seq, LANES), lambda b, p, i: (b, COL_VB * blk + p)),
            pl.BlockSpec((tq, LANES), lambda b, p, i: (qrow(b, p, i), COL_GB * blk + p)),
            pl.BlockSpec((tq, AUG_Q), lambda b, p, i: (qrow(b, p, i), p)),
            pl.BlockSpec((seq, AUG_K), lambda b, p, i: (b, p)),
        ],
        out_specs=pl.BlockSpec((tq, LANES), lambda b, p, i: (qrow(b, p, i), p)),
        out_shape=jax.ShapeDtypeStruct((m, WIDTH_B), BF16),
        scratch_shapes=[
            pltpu.VMEM((seq, LANES + AUG_K), BF16),
            pltpu.VMEM((2, seq // KEY_TILE, ACC_ROWS, KEY_TILE), BF16),
            pltpu.VMEM((2, LANES + LANES, tq), BF16),
            pltpu.VMEM((2, 1, tq), F32),
            pltpu.VMEM((2, ACC_ROWS, tq), F32),
            pltpu.VMEM((2, KEY_TILE, tq), F32),
            pltpu.VMEM((2, KEY_TILE, tq), F32),
        ],
        name="fox_attention",
        compiler_params=pltpu.CompilerParams(
            dimension_semantics=("parallel", "parallel", "arbitrary"),
            vmem_limit_bytes=VMEM_LIMIT),
    )(z, z, z, z, qaug, kaug)


def _outproj_kernel(ya_ref, yb_ref, x_ref, gate_ref, wa_ref, wb_ref, fg_ref, o_ref, *, final):
    y = jnp.dot(ya_ref[...], wa_ref[...], preferred_element_type=F32)
    y = y + jnp.dot(yb_ref[...], wb_ref[...], preferred_element_type=F32)
    x = x_ref[...] + gate_ref[...] * y
    if final:
        x = x * lax.rsqrt(jnp.mean(x * x, axis=-1, keepdims=True) + EPS) * fg_ref[...]
    o_ref[...] = x


def _outproj(ya, yb, x2, gate, w_out, final_g, seq, final):
    m, d = x2.shape
    tm = 512
    per_batch = seq // tm
    return pl.pallas_call(
        functools.partial(_outproj_kernel, final=final),
        grid=(m // tm,),
        in_specs=[
            pl.BlockSpec((tm, WIDTH_A), lambda i: (i, 0)),
            pl.BlockSpec((tm, WIDTH_B), lambda i: (i, 0)),
            pl.BlockSpec((tm, d), lambda i: (i, 0)),
            pl.BlockSpec((None, 1, d), lambda i: (i // per_batch, 0, 0)),
            pl.BlockSpec((WIDTH_A, d), lambda i: (0, 0)),
            pl.BlockSpec((WIDTH_B, d), lambda i: (1, 0)),
            pl.BlockSpec((1, d), lambda i: (0, 0)),
        ],
        out_specs=pl.BlockSpec((tm, d), lambda i: (i, 0)),
        out_shape=jax.ShapeDtypeStruct((m, d), F32),
        name="outproj_final" if final else "outproj",
        compiler_params=pltpu.CompilerParams(
            dimension_semantics=("parallel",), vmem_limit_bytes=VMEM_LIMIT),
    )(ya, yb, x2, gate, w_out, w_out, final_g)


def kernel(x, c, norm_g, w_ada, b_ada, w_in, ln_v_g, ln_v_b, w_s, b_s, b_f, w_out, final_g):
    batch, seq, d = x.shape
    depth = w_in.shape[0]
    c_pad = jnp.pad(c, ((0, 8 - batch), (0, 0)))
    mod = _ada_modulation(c_pad, w_ada, b_ada)[:, :batch]
    mod = mod.reshape(depth, batch, 3, 1, d)
    w_in_bf = w_in.astype(BF16)
    w_out_bf = w_out.astype(BF16)
    x2 = x.reshape(batch * seq, d)
    for l in range(depth):
        shift, scale, gate = mod[l, :, 0], mod[l, :, 1], mod[l, :, 2]
        w_f = jnp.pad(w_in_bf[l, :, N_MAIN:], ((0, 0), (0, LANES - HEADS_B)))
        z, f = _inproj(x2, shift, scale, norm_g[l][None], w_in_bf[l], w_f, seq)
        b_f_pad = jnp.pad(b_f[l], (0, LANES - HEADS_B))[None]
        qaug, kaug = _forget_bias(f, b_f_pad, seq)
        ya = _spatial_gating(z, ln_v_g[l][None], ln_v_b[l][None], w_s[l], b_s[l].T)
        yb = _attention(z, qaug, kaug, batch, seq)
        x2 = _outproj(ya, yb, x2, gate, w_out_bf[l], final_g[None], seq, final=(l == depth - 1))
    return x2.reshape(batch, seq, d)
```

```python
import functools

import jax
import jax.numpy as jnp
import numpy as np
from jax import lax
from jax.experimental import pallas as pl
from jax.experimental.pallas import tpu as pltpu

D_MODEL = 2048
WIDTH_A = 1024
WIDTH_B = 1024
CHUNK = 128
HEADS_A = 8
DH_A = WIDTH_A // HEADS_A
HEADS_B = 16
DH_B = WIDTH_B // HEADS_B
EPS = 1e-6

LANES = 128
HEAD_PAIRS = HEADS_B // 2
BIAS_LANES = 6
N_MAIN = 3 * WIDTH_A + 4 * WIDTH_B
VMEM_LIMIT = 56 * 1024 * 1024

COL_TILE = 256
QUERY_TILE = 1024
KEY_BLOCK = QUERY_TILE
ONES_ROWS = 16
ACC_ROWS = DH_B + ONES_ROWS

LOG2E = np.float32(1.0 / np.log(2.0))

COL_U, COL_V, COL_GA, COL_Q, COL_K, COL_VB, COL_GB = range(7)

BF16 = jnp.bfloat16
F32 = jnp.float32


def _silu(x):
    return x / (1.0 + jnp.exp(-x))


def _gelu(x):
    return 0.5 * x * (1.0 + lax.erf(x * np.float32(1.0 / np.sqrt(2.0))))


def _ada_kernel(c_ref, w_ref, b_ref, o_ref):
    cond = _silu(c_ref[...]).astype(BF16)
    o_ref[...] = jnp.dot(cond, w_ref[...].astype(BF16), preferred_element_type=F32) + b_ref[...]


def _ada_modulation(c_pad, w_ada, b_ada):
    depth, d, n = w_ada.shape
    tn = 1536
    rows = c_pad.shape[0]
    return pl.pallas_call(
        _ada_kernel,
        grid=(depth, n // tn),
        in_specs=[
            pl.BlockSpec((rows, d), lambda l, j: (0, 0)),
            pl.BlockSpec((None, d, tn), lambda l, j: (l, 0, j)),
            pl.BlockSpec((None, 1, tn), lambda l, j: (l, 0, j)),
        ],
        out_specs=pl.BlockSpec((None, rows, tn), lambda l, j: (l, 0, j)),
        out_shape=jax.ShapeDtypeStruct((depth, rows, n), F32),
        name="ada_modulation",
        compiler_params=pltpu.CompilerParams(
            dimension_semantics=("arbitrary", "arbitrary"), vmem_limit_bytes=VMEM_LIMIT),
    )(c_pad, w_ada, b_ada.reshape(depth, 1, n))


def _inproj_kernel(x_ref, shift_ref, scale_ref, g_ref, w_ref, wf_ref, z_ref, f_ref, h_ref):
    j = pl.program_id(1)

    @pl.when(j == 0)
    def _():
        x = x_ref[...]
        y = x * lax.rsqrt(jnp.mean(x * x, axis=-1, keepdims=True) + EPS) * g_ref[...]
        h = (y * (1.0 + scale_ref[...]) + shift_ref[...]).astype(BF16)
        h_ref[...] = h
        f_ref[...] = jnp.dot(h, wf_ref[...], preferred_element_type=F32)

    acc = jnp.dot(h_ref[...], w_ref[...], preferred_element_type=F32)

    @pl.when(j <= COL_V)
    def _():
        z_ref[...] = _gelu(acc).astype(BF16)

    @pl.when((j == COL_GA) | (j == COL_GB))
    def _():
        z_ref[...] = _silu(acc).astype(BF16)

    @pl.when(j == COL_Q)
    def _():
        z_ref[...] = (acc * (DH_B ** -0.5 * LOG2E)).astype(BF16)

    @pl.when((j == COL_K) | (j == COL_VB))
    def _():
        z_ref[...] = acc.astype(BF16)


def _inproj(x2, shift, scale, g, w_main, w_f, seq):
    m, d = x2.shape
    tm, tn = 1024, 1024
    per_batch = seq // tm
    return pl.pallas_call(
        _inproj_kernel,
        grid=(m // tm, N_MAIN // tn),
        in_specs=[
            pl.BlockSpec((tm, d), lambda i, j: (i, 0)),
            pl.BlockSpec((None, 1, d), lambda i, j: (i // per_batch, 0, 0)),
            pl.BlockSpec((None, 1, d), lambda i, j: (i // per_batch, 0, 0)),
            pl.BlockSpec((1, d), lambda i, j: (0, 0)),
            pl.BlockSpec((d, tn), lambda i, j: (0, j)),
            pl.BlockSpec((d, LANES), lambda i, j: (0, 0)),
        ],
        out_specs=[
            pl.BlockSpec((tm, tn), lambda i, j: (i, j)),
            pl.BlockSpec((tm, LANES), lambda i, j: (i, 0)),
        ],
        out_shape=[
            jax.ShapeDtypeStruct((m, N_MAIN), BF16),
            jax.ShapeDtypeStruct((m, LANES), F32),
        ],
        scratch_shapes=[pltpu.VMEM((tm, d), BF16)],
        name="norm_inproj",
        compiler_params=pltpu.CompilerParams(
            dimension_semantics=("parallel", "arbitrary"), vmem_limit_bytes=VMEM_LIMIT),
    )(x2, shift, scale, g, w_main, w_f)


def _split3(x):
    hi = x.astype(BF16)
    r1 = x - hi.astype(F32)
    mid = r1.astype(BF16)
    lo = (r1 - mid.astype(F32)).astype(BF16)
    return hi, mid, lo


def _forget_kernel(f_ref, bf_ref, sel_ref, const_ref, qa_ref, ka_ref, carry_ref):
    t = pl.program_id(1)

    @pl.when(t == 0)
    def _():
        carry_ref[...] = jnp.zeros_like(carry_ref)

    x = f_ref[...] + bf_ref[...]
    log_f = -(jnp.maximum(-x, 0.0) + jnp.log1p(jnp.exp(-jnp.abs(x))))
    ts = x.shape[0]
    row = lax.broadcasted_iota(jnp.int32, (ts, ts), 0)
    col = lax.broadcasted_iota(jnp.int32, (ts, ts), 1)
    tri = (row >= col).astype(BF16)
    parts = jnp.dot(tri, jnp.concatenate(_split3(log_f), axis=1), preferred_element_type=F32)
    cum = (parts[:, :LANES] + parts[:, LANES:2 * LANES]) + parts[:, 2 * LANES:] + carry_ref[...]
    carry_ref[...] = cum[ts - 1:ts, :]
    pieces = jnp.concatenate(_split3(cum * LOG2E), axis=1)
    aug = jnp.dot(pieces, sel_ref[...], preferred_element_type=F32) + const_ref[...]
    qa_ref[...] = aug[:, :LANES].T.astype(BF16)
    ka_ref[...] = aug[:, LANES:].astype(BF16)


def _bias_selectors():
    sel = np.zeros((3 * LANES, 2 * LANES), np.float32)
    const = np.zeros((1, 2 * LANES), np.float32)
    for head in range(HEADS_B):
        base = BIAS_LANES * head
        for piece in range(3):
            sel[piece * LANES + head, base + piece] = 1.0
            const[0, base + 3 + piece] = 1.0
            const[0, LANES + base + piece] = 1.0
            sel[piece * LANES + head, LANES + base + 3 + piece] = -1.0
    return jnp.asarray(sel, BF16), jnp.asarray(const)


def _forget_bias(f, b_f_pad, seq):
    m = f.shape[0]
    ts = 256
    per_batch = seq // ts
    sel, const = _bias_selectors()
    fixed = lambda shape: pl.BlockSpec(shape, lambda b, t: (0, 0))
    return pl.pallas_call(
        _forget_kernel,
        grid=(m // seq, per_batch),
        in_specs=[
            pl.BlockSpec((ts, LANES), lambda b, t: (b * per_batch + t, 0)),
            fixed((1, LANES)), fixed(sel.shape), fixed(const.shape),
        ],
        out_specs=[
            pl.BlockSpec((LANES, ts), lambda b, t: (0, b * per_batch + t)),
            pl.BlockSpec((ts, LANES), lambda b, t: (b * per_batch + t, 0)),
        ],
        out_shape=[
            jax.ShapeDtypeStruct((LANES, m), BF16),
            jax.ShapeDtypeStruct((m, LANES), BF16),
        ],
        scratch_shapes=[pltpu.VMEM((1, LANES), F32)],
        name="forget_bias",
        compiler_params=pltpu.CompilerParams(
            dimension_semantics=("parallel", "arbitrary"), vmem_limit_bytes=VMEM_LIMIT),
    )(f, b_f_pad, sel, const)


def _sgu_kernel(u_ref, v_ref, ga_ref, lng_ref, lnb_ref, ws_ref, bst_ref, o_ref):
    rows = u_ref.shape[0]
    v = v_ref[...].astype(F32)
    mu = jnp.mean(v, axis=-1, keepdims=True)
    vc = v - mu
    var = jnp.mean(vc * vc, axis=-1, keepdims=True)
    vn = (vc * lax.rsqrt(var + EPS) * lng_ref[...] + lnb_ref[...]).astype(BF16)
    trow = lax.broadcasted_iota(jnp.int32, (CHUNK, CHUNK), 0)
    tcol = lax.broadcasted_iota(jnp.int32, (CHUNK, CHUNK), 1)
    causal = trow >= tcol
    for h in range(HEADS_A):
        w = jnp.where(causal, ws_ref[h], 0.0).astype(BF16)
        bias = bst_ref[:, h:h + 1]
        cols = slice(h * DH_A, (h + 1) * DH_A)
        for c in range(rows // CHUNK):
            rws = slice(c * CHUNK, (c + 1) * CHUNK)
            mixed = jnp.dot(w, vn[rws, cols], preferred_element_type=F32) + bias
            y = u_ref[rws, cols].astype(F32) * mixed * ga_ref[rws, cols].astype(F32)
            o_ref[rws, cols] = y.astype(BF16)


def _spatial_gating(z, ln_g, ln_b, w_s, b_s_t):
    m = z.shape[0]
    tm = 512
    zcol = lambda col: pl.BlockSpec((tm, WIDTH_A), lambda i: (i, col))
    const2 = lambda shape: pl.BlockSpec(shape, lambda i: (0, 0))
    return pl.pallas_call(
        _sgu_kernel,
        grid=(m // tm,),
        in_specs=[
            zcol(COL_U), zcol(COL_V), zcol(COL_GA),
            const2((1, WIDTH_A)), const2((1, WIDTH_A)),
            pl.BlockSpec((HEADS_A, CHUNK, CHUNK), lambda i: (0, 0, 0)),
            const2((CHUNK, HEADS_A)),
        ],
        out_specs=pl.BlockSpec((tm, WIDTH_A), lambda i: (i, 0)),
        out_shape=jax.ShapeDtypeStruct((m, WIDTH_A), BF16),
        name="spatial_gating",
        compiler_params=pltpu.CompilerParams(
            dimension_semantics=("parallel",), vmem_limit_bytes=VMEM_LIMIT),
    )(z, z, z, ln_g, ln_b, w_s, b_s_t)


def _attn_kernel(q_ref, k_ref, v_ref, gb_ref, qa_ref, ka_ref, o_ref,
                 kk_ref, vt_ref, qt_ref, m_ref, acc_ref, s0_ref, s1_ref):
    pair = pl.program_id(1)
    qi = pl.program_id(2)
    tq = q_ref.shape[0]
    n_key_blocks = k_ref.shape[0] // KEY_BLOCK
    n_col = tq // COL_TILE

    @pl.when(qi == 0)
    def _():
        kk_ref[:, :LANES] = k_ref[...]
        kk_ref[:, LANES:] = ka_ref[...]
        ones = jnp.ones((ONES_ROWS, KEY_BLOCK), BF16)
        for j in range(n_key_blocks):
            vt = v_ref[j * KEY_BLOCK:(j + 1) * KEY_BLOCK, :].T
            vt_ref[0, j] = jnp.concatenate([vt[:DH_B], ones], axis=0)
            vt_ref[1, j] = jnp.concatenate([vt[DH_B:], ones], axis=0)

    def tile(c):
        return slice(c * COL_TILE, (c + 1) * COL_TILE)

    def prepare(c):
        q_t = q_ref[tile(c), :].T
        qa_t = qa_ref[:, tile(c)].astype(F32)
        bias_row = lax.broadcasted_iota(jnp.int32, qa_t.shape, 0)
        zeros = jnp.zeros((DH_B, COL_TILE), BF16)
        for h in range(2):
            first = BIAS_LANES * (2 * pair + h)
            own = (bias_row >= first) & (bias_row < first + BIAS_LANES)
            bias_t = jnp.where(own, qa_t, 0.0).astype(BF16)
            head_rows = [q_t[:DH_B], zeros] if h == 0 else [zeros, q_t[DH_B:]]
            qt_ref[h, :, tile(c)] = jnp.concatenate(head_rows + [bias_t], axis=0)

    def finish(c):
        o_t = jnp.concatenate(
            [acc_ref[h, :DH_B, tile(c)] / acc_ref[h, DH_B:DH_B + 1, tile(c)] for h in range(2)],
            axis=0)
        o_ref[tile(c), :] = (o_t.T * gb_ref[tile(c), :].astype(F32)).astype(BF16)

    order = tuple(reversed(range(n_col)))
    for c in order:
        prepare(c)
    m_ref[...] = jnp.full(m_ref.shape, -jnp.inf, F32)
    acc_ref[...] = jnp.zeros(acc_ref.shape, F32)

    def scores(j, c, n_keys, s_ref):
        kk = kk_ref[pl.ds(pl.multiple_of(j * KEY_BLOCK, KEY_BLOCK), n_keys), :]
        for h in range(2):
            s_ref[h, :n_keys, :] = jnp.dot(kk, qt_ref[h, :, tile(c)], preferred_element_type=F32)

    def softmax_pv(j, c, n_keys, s_ref, masked):
        cols = tile(c)
        for h in range(2):
            s_t = s_ref[h, :n_keys, :]
            if masked:
                key = lax.broadcasted_iota(jnp.int32, s_t.shape, 0)
                qry = lax.broadcasted_iota(jnp.int32, s_t.shape, 1) + c * COL_TILE
                s_t = jnp.where(key <= qry, s_t, -jnp.inf)
            m_old = m_ref[h, :, cols]
            m_new = jnp.maximum(m_old, jnp.max(s_t, axis=0, keepdims=True))
            alpha = jnp.exp2(m_old - m_new)
            p_t = jnp.exp2(s_t - m_new).astype(BF16)
            pv = jnp.dot(vt_ref[h, j, :, :n_keys], p_t, preferred_element_type=F32)
            acc_ref[h, :, cols] = alpha * acc_ref[h, :, cols] + pv
            m_ref[h, :, cols] = m_new

    s_bufs = (s0_ref, s1_ref)
    scores(0, order[0], KEY_BLOCK, s0_ref)

    def body(j, carry):
        for n, c in enumerate(order):
            nxt = (j, order[n + 1]) if n + 1 < n_col else (j + 1, order[0])
            scores(nxt[0], nxt[1], KEY_BLOCK, s_bufs[(n + 1) % 2])
            softmax_pv(j, c, KEY_BLOCK, s_bufs[n % 2], False)
        return carry

    lax.fori_loop(0, qi, body, 0)
    for n, c in enumerate(order):
        if n + 1 < n_col:
            scores(qi, order[n + 1], (order[n + 1] + 1) * COL_TILE, s_bufs[(n + 1) % 2])
        softmax_pv(qi, c, (c + 1) * COL_TILE, s_bufs[n % 2], True)
        finish(c)


def _attention(z, qaug_t, kaug, batch, seq):
    m = z.shape[0]
    tq = QUERY_TILE
    nq = seq // tq
    blk = WIDTH_B // LANES
    qrow = lambda b, p, i: b * nq + i
    return pl.pallas_call(
        _attn_kernel,
        grid=(batch, HEAD_PAIRS, nq),
        in_specs=[
            pl.BlockSpec((tq, LANES), lambda b, p, i: (qrow(b, p, i), COL_Q * blk + p)),
            pl.BlockSpec((seq, LANES), lambda b, p, i: (b, COL_K * blk + p)),
            pl.BlockSpec((seq, LANES), lambda b, p, i: (b, COL_VB * blk + p)),
            pl.BlockSpec((tq, LANES), lambda b, p, i: (qrow(b, p, i), COL_GB * blk + p)),
            pl.BlockSpec((LANES, tq), lambda b, p, i: (0, qrow(b, p, i))),
            pl.BlockSpec((seq, LANES), lambda b, p, i: (b, 0)),
        ],
        out_specs=pl.BlockSpec((tq, LANES), lambda b, p, i: (qrow(b, p, i), p)),
        out_shape=jax.ShapeDtypeStruct((m, WIDTH_B), BF16),
        scratch_shapes=[
            pltpu.VMEM((seq, 2 * LANES), BF16),
            pltpu.VMEM((2, seq // KEY_BLOCK, ACC_ROWS, KEY_BLOCK), BF16),
            pltpu.VMEM((2, 2 * LANES, tq), BF16),
            pltpu.VMEM((2, 1, tq), F32),
            pltpu.VMEM((2, ACC_ROWS, tq), F32),
            pltpu.VMEM((2, KEY_BLOCK, COL_TILE), F32),
            pltpu.VMEM((2, KEY_BLOCK, COL_TILE), F32),
        ],
        name="fox_attention",
        compiler_params=pltpu.CompilerParams(
            dimension_semantics=("parallel", "parallel", "arbitrary"),
            vmem_limit_bytes=VMEM_LIMIT),
    )(z, z, z, z, qaug_t, kaug)


def _outproj_kernel(ya_ref, yb_ref, x_ref, gate_ref, wa_ref, wb_ref, fg_ref, o_ref, *, final):
    y = jnp.dot(ya_ref[...], wa_ref[...], preferred_element_type=F32)
    y = y + jnp.dot(yb_ref[...], wb_ref[...], preferred_element_type=F32)
    x = x_ref[...] + gate_ref[...] * y
    if final:
        x = x * lax.rsqrt(jnp.mean(x * x, axis=-1, keepdims=True) + EPS) * fg_ref[...]
    o_ref[...] = x


def _outproj(ya, yb, x2, gate, w_out, final_g, seq, final):
    m, d = x2.shape
    tm = 512
    per_batch = seq // tm
    return pl.pallas_call(
        functools.partial(_outproj_kernel, final=final),
        grid=(m // tm,),
        in_specs=[
            pl.BlockSpec((tm, WIDTH_A), lambda i: (i, 0)),
            pl.BlockSpec((tm, WIDTH_B), lambda i: (i, 0)),
            pl.BlockSpec((tm, d), lambda i: (i, 0)),
            pl.BlockSpec((None, 1, d), lambda i: (i // per_batch, 0, 0)),
            pl.BlockSpec((WIDTH_A, d), lambda i: (0, 0)),
            pl.BlockSpec((WIDTH_B, d), lambda i: (1, 0)),
            pl.BlockSpec((1, d), lambda i: (0, 0)),
        ],
        out_specs=pl.BlockSpec((tm, d), lambda i: (i, 0)),
        out_shape=jax.ShapeDtypeStruct((m, d), F32),
        name="outproj_final" if final else "outproj",
        compiler_params=pltpu.CompilerParams(
            dimension_semantics=("parallel",), vmem_limit_bytes=VMEM_LIMIT),
    )(ya, yb, x2, gate, w_out, w_out, final_g)


def kernel(x, c, norm_g, w_ada, b_ada, w_in, ln_v_g, ln_v_b, w_s, b_s, b_f, w_out, final_g):
    batch, seq, d = x.shape
    depth = w_in.shape[0]
    c_pad = jnp.pad(c, ((0, 8 - batch), (0, 0)))
    mod = _ada_modulation(c_pad, w_ada, b_ada)[:, :batch]
    mod = mod.reshape(depth, batch, 3, 1, d)
    w_in_bf = w_in.astype(BF16)
    w_out_bf = w_out.astype(BF16)
    x2 = x.reshape(batch * seq, d)
    for l in range(depth):
        shift, scale, gate = mod[l, :, 0], mod[l, :, 1], mod[l, :, 2]
        w_f = jnp.pad(w_in_bf[l, :, N_MAIN:], ((0, 0), (0, LANES - HEADS_B)))
        z, f = _inproj(x2, shift, scale, norm_g[l][None], w_in_bf[l], w_f, seq)
        b_f_pad = jnp.pad(b_f[l], (0, LANES - HEADS_B))[None]
        qaug_t, kaug = _forget_bias(f, b_f_pad, seq)
        ya = _spatial_gating(z, ln_v_g[l][None], ln_v_b[l][None], w_s[l], b_s[l].T)
        yb = _attention(z, qaug_t, kaug, batch, seq)
        x2 = _outproj(ya, yb, x2, gate, w_out_bf[l], final_g[None], seq, final=(l == depth - 1))
    return x2.reshape(batch, seq, d)
```

```python
import functools

import jax
import jax.numpy as jnp
import numpy as np
from jax import lax
from jax.experimental import pallas as pl
from jax.experimental.pallas import tpu as pltpu

D_MODEL = 2048
WIDTH_A = 1024
WIDTH_B = 1024
CHUNK = 128
HEADS_A = 8
DH_A = WIDTH_A // HEADS_A
HEADS_B = 16
DH_B = WIDTH_B // HEADS_B
EPS = 1e-6

LANES = 128
HEAD_PAIRS = HEADS_B // 2
BIAS_LANES = 6
N_MAIN = 3 * WIDTH_A + 4 * WIDTH_B
VMEM_LIMIT = 56 * 1024 * 1024

PROJ_SUB = 256
COL_TILE = 256
KEY_CHUNK = 1024
ONES_ROWS = 16
ACC_ROWS = DH_B + ONES_ROWS

LOG2E = np.float32(1.0 / np.log(2.0))

COL_U, COL_V, COL_GA, COL_Q, COL_K, COL_VB, COL_GB = range(7)

BF16 = jnp.bfloat16
F32 = jnp.float32


def _silu(x):
    return x / (1.0 + jnp.exp(-x))


def _gelu(x):
    return 0.5 * x * (1.0 + lax.erf(x * np.float32(1.0 / np.sqrt(2.0))))


def _ada_kernel(c_ref, w_ref, b_ref, o_ref):
    cond = _silu(c_ref[...]).astype(BF16)
    o_ref[...] = jnp.dot(cond, w_ref[...].astype(BF16), preferred_element_type=F32) + b_ref[...]


def _ada_modulation(c_pad, w_ada, b_ada):
    depth, d, n = w_ada.shape
    tn = 1536
    rows = c_pad.shape[0]
    return pl.pallas_call(
        _ada_kernel,
        grid=(depth, n // tn),
        in_specs=[
            pl.BlockSpec((rows, d), lambda l, j: (0, 0)),
            pl.BlockSpec((None, d, tn), lambda l, j: (l, 0, j)),
            pl.BlockSpec((None, 1, tn), lambda l, j: (l, 0, j)),
        ],
        out_specs=pl.BlockSpec((None, rows, tn), lambda l, j: (l, 0, j)),
        out_shape=jax.ShapeDtypeStruct((depth, rows, n), F32),
        name="ada_modulation",
        compiler_params=pltpu.CompilerParams(
            dimension_semantics=("arbitrary", "arbitrary"), vmem_limit_bytes=VMEM_LIMIT),
    )(c_pad, w_ada, b_ada.reshape(depth, 1, n))


def _inproj_kernel(x_ref, shift_ref, scale_ref, g_ref, w_ref, wf_ref, z_ref, f_ref, h_ref):
    j = pl.program_id(1)

    @pl.when(j == 0)
    def _():
        x = x_ref[...]
        y = x * lax.rsqrt(jnp.mean(x * x, axis=-1, keepdims=True) + EPS) * g_ref[...]
        h = (y * (1.0 + scale_ref[...]) + shift_ref[...]).astype(BF16)
        h_ref[...] = h
        f_ref[...] = jnp.dot(h, wf_ref[...], preferred_element_type=F32)

    def project(act):
        for c in range(z_ref.shape[1] // PROJ_SUB):
            cols = slice(c * PROJ_SUB, (c + 1) * PROJ_SUB)
            acc = jnp.dot(h_ref[...], w_ref[:, cols], preferred_element_type=F32)
            z_ref[:, cols] = act(acc).astype(BF16)

    @pl.when(j <= COL_V)
    def _():
        project(_gelu)

    @pl.when((j == COL_GA) | (j == COL_GB))
    def _():
        project(_silu)

    @pl.when(j == COL_Q)
    def _():
        project(lambda acc: acc * (DH_B ** -0.5 * LOG2E))

    @pl.when((j == COL_K) | (j == COL_VB))
    def _():
        project(lambda acc: acc)


def _inproj(x2, shift, scale, g, w_in, w_f, layer, seq):
    m, d = x2.shape
    tm, tn = 1024, 1024
    per_batch = seq // tm
    return pl.pallas_call(
        _inproj_kernel,
        grid=(m // tm, N_MAIN // tn),
        in_specs=[
            pl.BlockSpec((tm, d), lambda i, j: (i, 0)),
            pl.BlockSpec((None, 1, d), lambda i, j: (i // per_batch, 0, 0)),
            pl.BlockSpec((None, 1, d), lambda i, j: (i // per_batch, 0, 0)),
            pl.BlockSpec((1, d), lambda i, j: (0, 0)),
            pl.BlockSpec((None, d, tn), lambda i, j: (layer, 0, j)),
            pl.BlockSpec((d, LANES), lambda i, j: (0, 0)),
        ],
        out_specs=[
            pl.BlockSpec((tm, tn), lambda i, j: (i, j)),
            pl.BlockSpec((tm, LANES), lambda i, j: (i, 0)),
        ],
        out_shape=[
            jax.ShapeDtypeStruct((m, N_MAIN), BF16),
            jax.ShapeDtypeStruct((m, LANES), F32),
        ],
        scratch_shapes=[pltpu.VMEM((tm, d), BF16)],
        name="norm_inproj",
        compiler_params=pltpu.CompilerParams(
            dimension_semantics=("parallel", "arbitrary"), vmem_limit_bytes=VMEM_LIMIT),
    )(x2, shift, scale, g, w_in, w_f)


def _split3(x):
    hi = x.astype(BF16)
    r1 = x - hi.astype(F32)
    mid = r1.astype(BF16)
    lo = (r1 - mid.astype(F32)).astype(BF16)
    return hi, mid, lo


def _forget_kernel(f_ref, bf_ref, sel_ref, const_ref, qa_ref, ka_ref, carry_ref):
    t = pl.program_id(1)

    @pl.when(t == 0)
    def _():
        carry_ref[...] = jnp.zeros_like(carry_ref)

    x = f_ref[...] + bf_ref[...]
    log_f = -(jnp.maximum(-x, 0.0) + jnp.log1p(jnp.exp(-jnp.abs(x))))
    ts = x.shape[0]
    row = lax.broadcasted_iota(jnp.int32, (ts, ts), 0)
    col = lax.broadcasted_iota(jnp.int32, (ts, ts), 1)
    tri = (row >= col).astype(BF16)
    parts = jnp.dot(tri, jnp.concatenate(_split3(log_f), axis=1), preferred_element_type=F32)
    cum = (parts[:, :LANES] + parts[:, LANES:2 * LANES]) + parts[:, 2 * LANES:] + carry_ref[...]
    carry_ref[...] = cum[ts - 1:ts, :]
    pieces = jnp.concatenate(_split3(cum * LOG2E), axis=1)
    aug = jnp.dot(pieces, sel_ref[...], preferred_element_type=F32) + const_ref[...]
    qa_ref[...] = aug[:, :LANES].T.astype(BF16)
    ka_ref[...] = aug[:, LANES:].astype(BF16)


def _bias_selectors():
    sel = np.zeros((3 * LANES, 2 * LANES), np.float32)
    const = np.zeros((1, 2 * LANES), np.float32)
    for head in range(HEADS_B):
        base = BIAS_LANES * head
        for piece in range(3):
            sel[piece * LANES + head, base + piece] = 1.0
            const[0, base + 3 + piece] = 1.0
            const[0, LANES + base + piece] = 1.0
            sel[piece * LANES + head, LANES + base + 3 + piece] = -1.0
    return jnp.asarray(sel, BF16), jnp.asarray(const)


def _forget_bias(f, b_f_pad, seq):
    m = f.shape[0]
    ts = 256
    per_batch = seq // ts
    sel, const = _bias_selectors()
    fixed = lambda shape: pl.BlockSpec(shape, lambda b, t: (0, 0))
    return pl.pallas_call(
        _forget_kernel,
        grid=(m // seq, per_batch),
        in_specs=[
            pl.BlockSpec((ts, LANES), lambda b, t: (b * per_batch + t, 0)),
            fixed((1, LANES)), fixed(sel.shape), fixed(const.shape),
        ],
        out_specs=[
            pl.BlockSpec((LANES, ts), lambda b, t: (0, b * per_batch + t)),
            pl.BlockSpec((ts, LANES), lambda b, t: (b * per_batch + t, 0)),
        ],
        out_shape=[
            jax.ShapeDtypeStruct((LANES, m), BF16),
            jax.ShapeDtypeStruct((m, LANES), BF16),
        ],
        scratch_shapes=[pltpu.VMEM((1, LANES), F32)],
        name="forget_bias",
        compiler_params=pltpu.CompilerParams(
            dimension_semantics=("parallel", "arbitrary"), vmem_limit_bytes=VMEM_LIMIT),
    )(f, b_f_pad, sel, const)


def _sgu_kernel(u_ref, v_ref, ga_ref, lng_ref, lnb_ref, ws_ref, bst_ref, o_ref):
    rows = u_ref.shape[0]
    v = v_ref[...].astype(F32)
    mu = jnp.mean(v, axis=-1, keepdims=True)
    vc = v - mu
    var = jnp.mean(vc * vc, axis=-1, keepdims=True)
    vn = (vc * lax.rsqrt(var + EPS) * lng_ref[...] + lnb_ref[...]).astype(BF16)
    trow = lax.broadcasted_iota(jnp.int32, (CHUNK, CHUNK), 0)
    tcol = lax.broadcasted_iota(jnp.int32, (CHUNK, CHUNK), 1)
    causal = trow >= tcol
    for h in range(HEADS_A):
        w = jnp.where(causal, ws_ref[h], 0.0).astype(BF16)
        bias = bst_ref[:, h:h + 1]
        cols = slice(h * DH_A, (h + 1) * DH_A)
        for c in range(rows // CHUNK):
            rws = slice(c * CHUNK, (c + 1) * CHUNK)
            mixed = jnp.dot(w, vn[rws, cols], preferred_element_type=F32) + bias
            y = u_ref[rws, cols].astype(F32) * mixed * ga_ref[rws, cols].astype(F32)
            o_ref[rws, cols] = y.astype(BF16)


def _spatial_gating(z, ln_g, ln_b, w_s, b_s_t):
    m = z.shape[0]
    tm = 512
    zcol = lambda col: pl.BlockSpec((tm, WIDTH_A), lambda i: (i, col))
    const2 = lambda shape: pl.BlockSpec(shape, lambda i: (0, 0))
    return pl.pallas_call(
        _sgu_kernel,
        grid=(m // tm,),
        in_specs=[
            zcol(COL_U), zcol(COL_V), zcol(COL_GA),
            const2((1, WIDTH_A)), const2((1, WIDTH_A)),
            pl.BlockSpec((HEADS_A, CHUNK, CHUNK), lambda i: (0, 0, 0)),
            const2((CHUNK, HEADS_A)),
        ],
        out_specs=pl.BlockSpec((tm, WIDTH_A), lambda i: (i, 0)),
        out_shape=jax.ShapeDtypeStruct((m, WIDTH_A), BF16),
        name="spatial_gating",
        compiler_params=pltpu.CompilerParams(
            dimension_semantics=("parallel",), vmem_limit_bytes=VMEM_LIMIT),
    )(z, z, z, ln_g, ln_b, w_s, b_s_t)


def _attn_kernel(q_ref, k_ref, v_ref, gb_ref, qa_ref, ka_ref, o_ref,
                 kk_ref, vt_ref, qt_ref, s0_ref, s1_ref, mx0_ref, mx1_ref):
    pair = pl.program_id(1)
    seq = q_ref.shape[0]
    n_col = seq // COL_TILE

    def tile(c):
        return slice(c * COL_TILE, (c + 1) * COL_TILE)

    kk_ref[:, :LANES] = k_ref[...]
    kk_ref[:, LANES:] = ka_ref[...]
    ones = jnp.ones((ONES_ROWS, KEY_CHUNK), BF16)
    for j in range(seq // KEY_CHUNK):
        rows = slice(j * KEY_CHUNK, (j + 1) * KEY_CHUNK)
        vt = v_ref[rows, :].T
        vt_ref[0, :, rows] = jnp.concatenate([vt[:DH_B], ones], axis=0)
        vt_ref[1, :, rows] = jnp.concatenate([vt[DH_B:], ones], axis=0)

    def prepare(c):
        q_t = q_ref[tile(c), :].T
        qa_t = qa_ref[:, tile(c)].astype(F32)
        bias_row = lax.broadcasted_iota(jnp.int32, qa_t.shape, 0)
        zeros = jnp.zeros((DH_B, COL_TILE), BF16)
        for h in range(2):
            first = BIAS_LANES * (2 * pair + h)
            own = (bias_row >= first) & (bias_row < first + BIAS_LANES)
            bias_t = jnp.where(own, qa_t, 0.0).astype(BF16)
            head_rows = [q_t[:DH_B], zeros] if h == 0 else [zeros, q_t[DH_B:]]
            qt_ref[h, :, tile(c)] = jnp.concatenate(head_rows + [bias_t], axis=0)

    def scores(c, buf):
        s_ref, mx_ref = buf
        below = c * COL_TILE
        for h in range(2):
            rhs = qt_ref[h, :, tile(c)]
            square = jnp.dot(kk_ref[tile(c), :], rhs, preferred_element_type=F32)
            key = lax.broadcasted_iota(jnp.int32, square.shape, 0)
            qry = lax.broadcasted_iota(jnp.int32, square.shape, 1)
            square = jnp.where(key <= qry, square, -jnp.inf)
            s_ref[h, tile(c), :] = square
            mx = jnp.max(square, axis=0, keepdims=True)
            if below:
                full = jnp.dot(kk_ref[:below, :], rhs, preferred_element_type=F32)
                s_ref[h, :below, :] = full
                mx = jnp.maximum(mx, jnp.max(full, axis=0, keepdims=True))
            mx_ref[h] = mx

    def softmax_pv(c, buf):
        s_ref, mx_ref = buf
        n_keys = (c + 1) * COL_TILE
        out_t = []
        for h in range(2):
            p_t = jnp.exp2(s_ref[h, :n_keys, :] - mx_ref[h]).astype(BF16)
            pv = jnp.dot(vt_ref[h, :, :n_keys], p_t, preferred_element_type=F32)
            out_t.append(pv[:DH_B] / pv[DH_B:DH_B + 1])
        o_t = jnp.concatenate(out_t, axis=0)
        o_ref[tile(c), :] = (o_t.T * gb_ref[tile(c), :].astype(F32)).astype(BF16)

    order = tuple(reversed(range(n_col)))
    for c in order:
        prepare(c)
    bufs = ((s0_ref, mx0_ref), (s1_ref, mx1_ref))
    scores(order[0], bufs[0])
    for n, c in enumerate(order):
        if n + 1 < n_col:
            scores(order[n + 1], bufs[(n + 1) % 2])
        softmax_pv(c, bufs[n % 2])


def _attention(z, qaug_t, kaug, batch, seq):
    m = z.shape[0]
    blk = WIDTH_B // LANES
    zblock = lambda col: pl.BlockSpec((seq, LANES), lambda b, p: (b, col * blk + p))
    return pl.pallas_call(
        _attn_kernel,
        grid=(batch, HEAD_PAIRS),
        in_specs=[
            zblock(COL_Q), zblock(COL_K), zblock(COL_VB), zblock(COL_GB),
            pl.BlockSpec((LANES, seq), lambda b, p: (0, b)),
            pl.BlockSpec((seq, LANES), lambda b, p: (b, 0)),
        ],
        out_specs=pl.BlockSpec((seq, LANES), lambda b, p: (b, p)),
        out_shape=jax.ShapeDtypeStruct((m, WIDTH_B), BF16),
        scratch_shapes=[
            pltpu.VMEM((seq, 2 * LANES), BF16),
            pltpu.VMEM((2, ACC_ROWS, seq), BF16),
            pltpu.VMEM((2, 2 * LANES, seq), BF16),
            pltpu.VMEM((2, seq, COL_TILE), F32),
            pltpu.VMEM((2, seq, COL_TILE), F32),
            pltpu.VMEM((2, 1, COL_TILE), F32),
            pltpu.VMEM((2, 1, COL_TILE), F32),
        ],
        name="fox_attention",
        compiler_params=pltpu.CompilerParams(
            dimension_semantics=("parallel", "parallel"), vmem_limit_bytes=VMEM_LIMIT),
    )(z, z, z, z, qaug_t, kaug)


def _outproj_kernel(ya_ref, yb_ref, x_ref, gate_ref, wa_ref, wb_ref, fg_ref, o_ref, *, final):
    y = jnp.dot(ya_ref[...], wa_ref[...], preferred_element_type=F32)
    y = y + jnp.dot(yb_ref[...], wb_ref[...], preferred_element_type=F32)
    x = x_ref[...] + gate_ref[...] * y
    if final:
        x = x * lax.rsqrt(jnp.mean(x * x, axis=-1, keepdims=True) + EPS) * fg_ref[...]
    o_ref[...] = x


def _outproj(ya, yb, x2, gate, w_out, final_g, layer, seq, final):
    m, d = x2.shape
    tm = 512
    per_batch = seq // tm
    return pl.pallas_call(
        functools.partial(_outproj_kernel, final=final),
        grid=(m // tm,),
        in_specs=[
            pl.BlockSpec((tm, WIDTH_A), lambda i: (i, 0)),
            pl.BlockSpec((tm, WIDTH_B), lambda i: (i, 0)),
            pl.BlockSpec((tm, d), lambda i: (i, 0)),
            pl.BlockSpec((None, 1, d), lambda i: (i // per_batch, 0, 0)),
            pl.BlockSpec((None, WIDTH_A, d), lambda i: (layer, 0, 0)),
            pl.BlockSpec((None, WIDTH_B, d), lambda i: (layer, 1, 0)),
            pl.BlockSpec((1, d), lambda i: (0, 0)),
        ],
        out_specs=pl.BlockSpec((tm, d), lambda i: (i, 0)),
        out_shape=jax.ShapeDtypeStruct((m, d), F32),
        name="outproj_final" if final else "outproj",
        compiler_params=pltpu.CompilerParams(
            dimension_semantics=("parallel",), vmem_limit_bytes=VMEM_LIMIT),
    )(ya, yb, x2, gate, w_out, w_out, final_g)


def kernel(x, c, norm_g, w_ada, b_ada, w_in, ln_v_g, ln_v_b, w_s, b_s, b_f, w_out, final_g):
    batch, seq, d = x.shape
    depth = w_in.shape[0]
    c_pad = jnp.pad(c, ((0, 8 - batch), (0, 0)))
    mod = _ada_modulation(c_pad, w_ada, b_ada)[:, :batch]
    mod = mod.reshape(depth, batch, 3, 1, d)
    w_in_bf = w_in.astype(BF16)
    w_out_bf = w_out.astype(BF16)
    x2 = x.reshape(batch * seq, d)
    for l in range(depth):
        shift, scale, gate = mod[l, :, 0], mod[l, :, 1], mod[l, :, 2]
        w_f = jnp.pad(w_in_bf[l, :, N_MAIN:], ((0, 0), (0, LANES - HEADS_B)))
        z, f = _inproj(x2, shift, scale, norm_g[l][None], w_in_bf, w_f, l, seq)
        b_f_pad = jnp.pad(b_f[l], (0, LANES - HEADS_B))[None]
        qaug_t, kaug = _forget_bias(f, b_f_pad, seq)
        ya = _spatial_gating(z, ln_v_g[l][None], ln_v_b[l][None], w_s[l], b_s[l].T)
        yb = _attention(z, qaug_t, kaug, batch, seq)
        x2 = _outproj(ya, yb, x2, gate, w_out_bf, final_g[None], l, seq, final=(l == depth - 1))
    return x2.reshape(batch, seq, d)
```

```python
import functools

import jax
import jax.numpy as jnp
import numpy as np
from jax import lax
from jax.experimental import pallas as pl
from jax.experimental.pallas import tpu as pltpu

D_MODEL = 2048
WIDTH_A = 1024
WIDTH_B = 1024
CHUNK = 128
HEADS_A = 8
DH_A = WIDTH_A // HEADS_A
HEADS_B = 16
DH_B = WIDTH_B // HEADS_B
EPS = 1e-6

LANES = 128
HEAD_PAIRS = HEADS_B // 2
BIAS_LANES = 6
N_MAIN = 3 * WIDTH_A + 4 * WIDTH_B
VMEM_LIMIT = 56 * 1024 * 1024

PROJ_SUB = 256
COL_TILE = 256
KEY_CHUNK = 1024
ONES_ROWS = 16
ACC_ROWS = DH_B + ONES_ROWS

LOG2E = np.float32(1.0 / np.log(2.0))

COL_U, COL_V, COL_GA, COL_Q, COL_K, COL_VB, COL_GB = range(7)

BF16 = jnp.bfloat16
F32 = jnp.float32


def _silu(x):
    return x / (1.0 + jnp.exp(-x))


def _gelu(x):
    return 0.5 * x * (1.0 + lax.erf(x * np.float32(1.0 / np.sqrt(2.0))))


def _ada_kernel(c_ref, w_ref, b_ref, o_ref):
    cond = _silu(c_ref[...]).astype(BF16)
    o_ref[...] = jnp.dot(cond, w_ref[...].astype(BF16), preferred_element_type=F32) + b_ref[...]


def _ada_modulation(c_pad, w_ada, b_ada):
    depth, d, n = w_ada.shape
    tn = 1536
    rows = c_pad.shape[0]
    return pl.pallas_call(
        _ada_kernel,
        grid=(depth, n // tn),
        in_specs=[
            pl.BlockSpec((rows, d), lambda l, j: (0, 0)),
            pl.BlockSpec((None, d, tn), lambda l, j: (l, 0, j)),
            pl.BlockSpec((None, 1, tn), lambda l, j: (l, 0, j)),
        ],
        out_specs=pl.BlockSpec((None, rows, tn), lambda l, j: (l, 0, j)),
        out_shape=jax.ShapeDtypeStruct((depth, rows, n), F32),
        name="ada_modulation",
        compiler_params=pltpu.CompilerParams(
            dimension_semantics=("arbitrary", "arbitrary"), vmem_limit_bytes=VMEM_LIMIT),
    )(c_pad, w_ada, b_ada.reshape(depth, 1, n))


def _inproj_kernel(x_ref, shift_ref, scale_ref, g_ref, w_ref, wf_ref, z_ref, f_ref, h_ref):
    j = pl.program_id(1)

    def normalize():
        x = x_ref[...]
        y = x * lax.rsqrt(jnp.mean(x * x, axis=-1, keepdims=True) + EPS) * g_ref[...]
        h = (y * (1.0 + scale_ref[...]) + shift_ref[...]).astype(BF16)
        h_ref[...] = h
        f_ref[...] = jnp.dot(h, wf_ref[...], preferred_element_type=F32)

    def project(act):
        for c in range(z_ref.shape[1] // PROJ_SUB):
            cols = slice(c * PROJ_SUB, (c + 1) * PROJ_SUB)
            acc = jnp.dot(h_ref[...], w_ref[:, cols].astype(BF16), preferred_element_type=F32)
            z_ref[:, cols] = act(acc).astype(BF16)

    @pl.when(j == 0)
    def _():
        normalize()
        project(_gelu)

    @pl.when(j == COL_V)
    def _():
        project(_gelu)

    @pl.when((j == COL_GA) | (j == COL_GB))
    def _():
        project(_silu)

    @pl.when(j == COL_Q)
    def _():
        project(lambda acc: acc * (DH_B ** -0.5 * LOG2E))

    @pl.when((j == COL_K) | (j == COL_VB))
    def _():
        project(lambda acc: acc)


def _inproj(x2, shift, scale, g, w_in, w_f, layer, seq):
    m, d = x2.shape
    tm, tn = 1024, 1024
    per_batch = seq // tm
    return pl.pallas_call(
        _inproj_kernel,
        grid=(m // tm, N_MAIN // tn),
        in_specs=[
            pl.BlockSpec((tm, d), lambda i, j: (i, 0)),
            pl.BlockSpec((None, 1, d), lambda i, j: (i // per_batch, 0, 0)),
            pl.BlockSpec((None, 1, d), lambda i, j: (i // per_batch, 0, 0)),
            pl.BlockSpec((1, d), lambda i, j: (0, 0)),
            pl.BlockSpec((None, d, tn), lambda i, j: (layer, 0, j)),
            pl.BlockSpec((d, LANES), lambda i, j: (0, 0)),
        ],
        out_specs=[
            pl.BlockSpec((tm, tn), lambda i, j: (i, j)),
            pl.BlockSpec((tm, LANES), lambda i, j: (i, 0)),
        ],
        out_shape=[
            jax.ShapeDtypeStruct((m, N_MAIN), BF16),
            jax.ShapeDtypeStruct((m, LANES), F32),
        ],
        scratch_shapes=[pltpu.VMEM((tm, d), BF16)],
        name="norm_inproj",
        compiler_params=pltpu.CompilerParams(
            dimension_semantics=("parallel", "arbitrary"), vmem_limit_bytes=VMEM_LIMIT),
    )(x2, shift, scale, g, w_in, w_f)


def _split3(x):
    hi = x.astype(BF16)
    r1 = x - hi.astype(F32)
    mid = r1.astype(BF16)
    lo = (r1 - mid.astype(F32)).astype(BF16)
    return hi, mid, lo


def _forget_kernel(f_ref, bf_ref, sel_ref, const_ref, qa_ref, ka_ref, carry_ref):
    t = pl.program_id(1)

    @pl.when(t == 0)
    def _():
        carry_ref[...] = jnp.zeros_like(carry_ref)

    x = f_ref[...] + bf_ref[...]
    log_f = -(jnp.maximum(-x, 0.0) + jnp.log1p(jnp.exp(-jnp.abs(x))))
    ts = x.shape[0]
    row = lax.broadcasted_iota(jnp.int32, (ts, ts), 0)
    col = lax.broadcasted_iota(jnp.int32, (ts, ts), 1)
    tri = (row >= col).astype(BF16)
    parts = jnp.dot(tri, jnp.concatenate(_split3(log_f), axis=1), preferred_element_type=F32)
    cum = (parts[:, :LANES] + parts[:, LANES:2 * LANES]) + parts[:, 2 * LANES:] + carry_ref[...]
    carry_ref[...] = cum[ts - 1:ts, :]
    pieces = jnp.concatenate(_split3(cum * LOG2E), axis=1)
    aug = jnp.dot(pieces, sel_ref[...], preferred_element_type=F32) + const_ref[...]
    qa_ref[...] = aug[:, :LANES].T.astype(BF16)
    ka_ref[...] = aug[:, LANES:].astype(BF16)


def _bias_selectors():
    sel = np.zeros((3 * LANES, 2 * LANES), np.float32)
    const = np.zeros((1, 2 * LANES), np.float32)
    for head in range(HEADS_B):
        base = BIAS_LANES * head
        for piece in range(3):
            sel[piece * LANES + head, base + piece] = 1.0
            const[0, base + 3 + piece] = 1.0
            const[0, LANES + base + piece] = 1.0
            sel[piece * LANES + head, LANES + base + 3 + piece] = -1.0
    return jnp.asarray(sel, BF16), jnp.asarray(const)


def _forget_bias(f, b_f_pad, seq):
    m = f.shape[0]
    ts = 512
    per_batch = seq // ts
    sel, const = _bias_selectors()
    fixed = lambda shape: pl.BlockSpec(shape, lambda b, t: (0, 0))
    return pl.pallas_call(
        _forget_kernel,
        grid=(m // seq, per_batch),
        in_specs=[
            pl.BlockSpec((ts, LANES), lambda b, t: (b * per_batch + t, 0)),
            fixed((1, LANES)), fixed(sel.shape), fixed(const.shape),
        ],
        out_specs=[
            pl.BlockSpec((LANES, ts), lambda b, t: (0, b * per_batch + t)),
            pl.BlockSpec((ts, LANES), lambda b, t: (b * per_batch + t, 0)),
        ],
        out_shape=[
            jax.ShapeDtypeStruct((LANES, m), BF16),
            jax.ShapeDtypeStruct((m, LANES), BF16),
        ],
        scratch_shapes=[pltpu.VMEM((1, LANES), F32)],
        name="forget_bias",
        compiler_params=pltpu.CompilerParams(
            dimension_semantics=("parallel", "arbitrary"), vmem_limit_bytes=VMEM_LIMIT),
    )(f, b_f_pad, sel, const)


def _spatial_gating(u_ref, v_ref, ga_ref, lng_ref, lnb_ref, ws_ref, bst_ref, o_ref):
    rows = u_ref.shape[0]
    v = v_ref[...].astype(F32)
    mu = jnp.mean(v, axis=-1, keepdims=True)
    vc = v - mu
    var = jnp.mean(vc * vc, axis=-1, keepdims=True)
    vn = (vc * lax.rsqrt(var + EPS) * lng_ref[...] + lnb_ref[...]).astype(BF16)
    trow = lax.broadcasted_iota(jnp.int32, (CHUNK, CHUNK), 0)
    tcol = lax.broadcasted_iota(jnp.int32, (CHUNK, CHUNK), 1)
    causal = trow >= tcol
    for h in range(HEADS_A):
        w = jnp.where(causal, ws_ref[h], 0.0).astype(BF16)
        bias = bst_ref[:, h:h + 1]
        cols = slice(h * DH_A, (h + 1) * DH_A)
        for c in range(rows // CHUNK):
            rws = slice(c * CHUNK, (c + 1) * CHUNK)
            mixed = jnp.dot(w, vn[rws, cols], preferred_element_type=F32) + bias
            y = u_ref[rws, cols].astype(F32) * mixed * ga_ref[rws, cols].astype(F32)
            o_ref[rws, cols] = y.astype(BF16)


def _attn_kernel(q_ref, k_ref, v_ref, gb_ref, qa_ref, ka_ref, o_ref,
                 kk_ref, vt_ref, qt_ref, s0_ref, s1_ref, mx0_ref, mx1_ref):
    pair = pl.program_id(1)
    seq = q_ref.shape[0]
    n_col = seq // COL_TILE

    def tile(c):
        return slice(c * COL_TILE, (c + 1) * COL_TILE)

    kk_ref[:, :LANES] = k_ref[...]
    kk_ref[:, LANES:] = ka_ref[...]
    ones = jnp.ones((ONES_ROWS, KEY_CHUNK), BF16)
    for j in range(seq // KEY_CHUNK):
        rows = slice(j * KEY_CHUNK, (j + 1) * KEY_CHUNK)
        vt = v_ref[rows, :].T
        vt_ref[0, :, rows] = jnp.concatenate([vt[:DH_B], ones], axis=0)
        vt_ref[1, :, rows] = jnp.concatenate([vt[DH_B:], ones], axis=0)

    def prepare(c):
        q_t = q_ref[tile(c), :].T
        qa_t = qa_ref[:, tile(c)].astype(F32)
        bias_row = lax.broadcasted_iota(jnp.int32, qa_t.shape, 0)
        zeros = jnp.zeros((DH_B, COL_TILE), BF16)
        for h in range(2):
            first = BIAS_LANES * (2 * pair + h)
            own = (bias_row >= first) & (bias_row < first + BIAS_LANES)
            bias_t = jnp.where(own, qa_t, 0.0).astype(BF16)
            head_rows = [q_t[:DH_B], zeros] if h == 0 else [zeros, q_t[DH_B:]]
            qt_ref[h, :, tile(c)] = jnp.concatenate(head_rows + [bias_t], axis=0)

    def scores(c, buf):
        s_ref, mx_ref = buf
        below = c * COL_TILE
        for h in range(2):
            rhs = qt_ref[h, :, tile(c)]
            square = jnp.dot(kk_ref[tile(c), :], rhs, preferred_element_type=F32)
            key = lax.broadcasted_iota(jnp.int32, square.shape, 0)
            qry = lax.broadcasted_iota(jnp.int32, square.shape, 1)
            square = jnp.where(key <= qry, square, -jnp.inf)
            s_ref[h, tile(c), :] = square
            mx = jnp.max(square, axis=0, keepdims=True)
            if below:
                full = jnp.dot(kk_ref[:below, :], rhs, preferred_element_type=F32)
                s_ref[h, :below, :] = full
                mx = jnp.maximum(mx, jnp.max(full, axis=0, keepdims=True))
            mx_ref[h] = mx

    def softmax_pv(c, buf):
        s_ref, mx_ref = buf
        n_keys = (c + 1) * COL_TILE
        out_t = []
        for h in range(2):
            p_t = jnp.exp2(s_ref[h, :n_keys, :] - mx_ref[h]).astype(BF16)
            pv = jnp.dot(vt_ref[h, :, :n_keys], p_t, preferred_element_type=F32)
            out_t.append(pv[:DH_B] / pv[DH_B:DH_B + 1])
        o_t = jnp.concatenate(out_t, axis=0)
        o_ref[tile(c), :] = (o_t.T * gb_ref[tile(c), :].astype(F32)).astype(BF16)

    order = tuple(reversed(range(n_col)))
    for c in order:
        prepare(c)
    bufs = ((s0_ref, mx0_ref), (s1_ref, mx1_ref))
    scores(order[0], bufs[0])
    for n, c in enumerate(order):
        if n + 1 < n_col:
            scores(order[n + 1], bufs[(n + 1) % 2])
        softmax_pv(c, bufs[n % 2])


def _attention(z, qaug_t, kaug, batch, seq):
    m = z.shape[0]
    blk = WIDTH_B // LANES
    zblock = lambda col: pl.BlockSpec((seq, LANES), lambda b, p: (b, col * blk + p))
    return pl.pallas_call(
        _attn_kernel,
        grid=(batch, HEAD_PAIRS),
        in_specs=[
            zblock(COL_Q), zblock(COL_K), zblock(COL_VB), zblock(COL_GB),
            pl.BlockSpec((LANES, seq), lambda b, p: (0, b)),
            pl.BlockSpec((seq, LANES), lambda b, p: (b, 0)),
        ],
        out_specs=pl.BlockSpec((seq, LANES), lambda b, p: (b, p)),
        out_shape=jax.ShapeDtypeStruct((m, WIDTH_B), BF16),
        scratch_shapes=[
            pltpu.VMEM((seq, 2 * LANES), BF16),
            pltpu.VMEM((2, ACC_ROWS, seq), BF16),
            pltpu.VMEM((2, 2 * LANES, seq), BF16),
            pltpu.VMEM((2, seq, COL_TILE), F32),
            pltpu.VMEM((2, seq, COL_TILE), F32),
            pltpu.VMEM((2, 1, COL_TILE), F32),
            pltpu.VMEM((2, 1, COL_TILE), F32),
        ],
        name="fox_attention",
        compiler_params=pltpu.CompilerParams(
            dimension_semantics=("parallel", "parallel"), vmem_limit_bytes=VMEM_LIMIT),
    )(z, z, z, z, qaug_t, kaug)


def _outproj_kernel(u_ref, v_ref, ga_ref, lng_ref, lnb_ref, ws_ref, bst_ref, yb_ref, x_ref,
                    gate_ref, wa_ref, wb_ref, fg_ref, o_ref, ya_ref, *, final):
    _spatial_gating(u_ref, v_ref, ga_ref, lng_ref, lnb_ref, ws_ref, bst_ref, ya_ref)
    y = jnp.dot(ya_ref[...], wa_ref[...], preferred_element_type=F32)
    y = y + jnp.dot(yb_ref[...], wb_ref[...], preferred_element_type=F32)
    x = x_ref[...] + gate_ref[...] * y
    if final:
        x = x * lax.rsqrt(jnp.mean(x * x, axis=-1, keepdims=True) + EPS) * fg_ref[...]
    o_ref[...] = x


def _outproj(z, sgu_params, yb, x2, gate, w_out, final_g, layer, seq, final):
    m, d = x2.shape
    tm = 512
    per_batch = seq // tm
    zcol = lambda col: pl.BlockSpec((tm, WIDTH_A), lambda i: (i, col))
    fixed = lambda shape: pl.BlockSpec(shape, lambda i: (0,) * len(shape))
    return pl.pallas_call(
        functools.partial(_outproj_kernel, final=final),
        grid=(m // tm,),
        in_specs=[
            zcol(COL_U), zcol(COL_V), zcol(COL_GA),
            fixed((1, WIDTH_A)), fixed((1, WIDTH_A)),
            fixed((HEADS_A, CHUNK, CHUNK)), fixed((CHUNK, HEADS_A)),
            pl.BlockSpec((tm, WIDTH_B), lambda i: (i, 0)),
            pl.BlockSpec((tm, d), lambda i: (i, 0)),
            pl.BlockSpec((None, 1, d), lambda i: (i // per_batch, 0, 0)),
            pl.BlockSpec((None, WIDTH_A, d), lambda i: (layer, 0, 0)),
            pl.BlockSpec((None, WIDTH_B, d), lambda i: (layer, 1, 0)),
            fixed((1, d)),
        ],
        out_specs=pl.BlockSpec((tm, d), lambda i: (i, 0)),
        out_shape=jax.ShapeDtypeStruct((m, d), F32),
        scratch_shapes=[pltpu.VMEM((tm, WIDTH_A), BF16)],
        name="outproj_final" if final else "outproj",
        compiler_params=pltpu.CompilerParams(
            dimension_semantics=("parallel",), vmem_limit_bytes=VMEM_LIMIT),
    )(z, z, z, *sgu_params, yb, x2, gate, w_out, w_out, final_g)


def kernel(x, c, norm_g, w_ada, b_ada, w_in, ln_v_g, ln_v_b, w_s, b_s, b_f, w_out, final_g):
    batch, seq, d = x.shape
    depth = w_in.shape[0]
    c_pad = jnp.pad(c, ((0, 8 - batch), (0, 0)))
    mod = _ada_modulation(c_pad, w_ada, b_ada)[:, :batch]
    mod = mod.reshape(depth, batch, 3, 1, d)
    w_out_bf = w_out.astype(BF16)
    x2 = x.reshape(batch * seq, d)
    for l in range(depth):
        shift, scale, gate = mod[l, :, 0], mod[l, :, 1], mod[l, :, 2]
        w_f = jnp.pad(w_in[l, :, N_MAIN:].astype(BF16), ((0, 0), (0, LANES - HEADS_B)))
        z, f = _inproj(x2, shift, scale, norm_g[l][None], w_in, w_f, l, seq)
        b_f_pad = jnp.pad(b_f[l], (0, LANES - HEADS_B))[None]
        qaug_t, kaug = _forget_bias(f, b_f_pad, seq)
        yb = _attention(z, qaug_t, kaug, batch, seq)
        sgu_params = (ln_v_g[l][None], ln_v_b[l][None], w_s[l], b_s[l].T)
        x2 = _outproj(z, sgu_params, yb, x2, gate, w_out_bf, final_g[None], l, seq,
                      final=(l == depth - 1))
    return x2.reshape(batch, seq, d)
```

```python
import functools

import jax
import jax.numpy as jnp
import numpy as np
from jax import lax
from jax.experimental import pallas as pl
from jax.experimental.pallas import tpu as pltpu

D_MODEL = 2048
WIDTH_A = 1024
WIDTH_B = 1024
CHUNK = 128
HEADS_A = 8
DH_A = WIDTH_A // HEADS_A
HEADS_B = 16
DH_B = WIDTH_B // HEADS_B
EPS = 1e-6

LANES = 128
HEAD_PAIRS = HEADS_B // 2
BIAS_LANES = 6
N_MAIN = 3 * WIDTH_A + 4 * WIDTH_B
VMEM_LIMIT = 56 * 1024 * 1024

PROJ_SUB = 256
COL_TILE = 256
KEY_CHUNK = 1024
ONES_ROWS = 16
ACC_ROWS = DH_B + ONES_ROWS

LOG2E = np.float32(1.0 / np.log(2.0))

COL_U, COL_V, COL_GA, COL_Q, COL_K, COL_VB, COL_GB = range(7)

BF16 = jnp.bfloat16
F32 = jnp.float32


def _silu(x):
    return x / (1.0 + jnp.exp(-x))


def _gelu(x):
    return 0.5 * x * (1.0 + lax.erf(x * np.float32(1.0 / np.sqrt(2.0))))


def _ada_kernel(c_ref, w_ref, b_ref, o_ref):
    cond = _silu(c_ref[...]).astype(BF16)
    o_ref[...] = jnp.dot(cond, w_ref[...].astype(BF16), preferred_element_type=F32) + b_ref[...]


def _ada_modulation(c_pad, w_ada, b_ada):
    depth, d, n = w_ada.shape
    tn = 1536
    rows = c_pad.shape[0]
    return pl.pallas_call(
        _ada_kernel,
        grid=(depth, n // tn),
        in_specs=[
            pl.BlockSpec((rows, d), lambda l, j: (0, 0)),
            pl.BlockSpec((None, d, tn), lambda l, j: (l, 0, j)),
            pl.BlockSpec((None, 1, tn), lambda l, j: (l, 0, j)),
        ],
        out_specs=pl.BlockSpec((None, rows, tn), lambda l, j: (l, 0, j)),
        out_shape=jax.ShapeDtypeStruct((depth, rows, n), F32),
        name="ada_modulation",
        compiler_params=pltpu.CompilerParams(
            dimension_semantics=("arbitrary", "arbitrary"), vmem_limit_bytes=VMEM_LIMIT),
    )(c_pad, w_ada, b_ada.reshape(depth, 1, n))


def _inproj_kernel(x_ref, shift_ref, scale_ref, g_ref, w_ref, wf_ref, z_ref, f_ref, h_ref):
    j = pl.program_id(1)

    def normalize():
        x = x_ref[...]
        y = x * lax.rsqrt(jnp.mean(x * x, axis=-1, keepdims=True) + EPS) * g_ref[...]
        h = (y * (1.0 + scale_ref[...]) + shift_ref[...]).astype(BF16)
        h_ref[...] = h
        f_ref[...] = jnp.dot(h, wf_ref[...], preferred_element_type=F32)

    def project(act):
        for c in range(z_ref.shape[1] // PROJ_SUB):
            cols = slice(c * PROJ_SUB, (c + 1) * PROJ_SUB)
            acc = jnp.dot(h_ref[...], w_ref[:, cols], preferred_element_type=F32)
            z_ref[:, cols] = act(acc).astype(BF16)

    @pl.when(j == 0)
    def _():
        normalize()
        project(_gelu)

    @pl.when(j == COL_V)
    def _():
        project(_gelu)

    @pl.when((j == COL_GA) | (j == COL_GB))
    def _():
        project(_silu)

    @pl.when(j == COL_Q)
    def _():
        project(lambda acc: acc * (DH_B ** -0.5 * LOG2E))

    @pl.when((j == COL_K) | (j == COL_VB))
    def _():
        project(lambda acc: acc)


def _inproj(x2, shift, scale, g, w_in, w_f, layer, seq):
    m, d = x2.shape
    tm, tn = 1024, 1024
    per_batch = seq // tm
    return pl.pallas_call(
        _inproj_kernel,
        grid=(m // tm, N_MAIN // tn),
        in_specs=[
            pl.BlockSpec((tm, d), lambda i, j: (i, 0)),
            pl.BlockSpec((None, 1, d), lambda i, j: (i // per_batch, 0, 0)),
            pl.BlockSpec((None, 1, d), lambda i, j: (i // per_batch, 0, 0)),
            pl.BlockSpec((1, d), lambda i, j: (0, 0)),
            pl.BlockSpec((None, d, tn), lambda i, j: (layer, 0, j)),
            pl.BlockSpec((d, LANES), lambda i, j: (0, 0)),
        ],
        out_specs=[
            pl.BlockSpec((tm, tn), lambda i, j: (i, j)),
            pl.BlockSpec((tm, LANES), lambda i, j: (i, 0)),
        ],
        out_shape=[
            jax.ShapeDtypeStruct((m, N_MAIN), BF16),
            jax.ShapeDtypeStruct((m, LANES), F32),
        ],
        scratch_shapes=[pltpu.VMEM((tm, d), BF16)],
        name="norm_inproj",
        compiler_params=pltpu.CompilerParams(
            dimension_semantics=("parallel", "arbitrary"), vmem_limit_bytes=VMEM_LIMIT),
    )(x2, shift, scale, g, w_in, w_f)


def _split3(x):
    hi = x.astype(BF16)
    r1 = x - hi.astype(F32)
    mid = r1.astype(BF16)
    lo = (r1 - mid.astype(F32)).astype(BF16)
    return hi, mid, lo


def _forget_kernel(f_ref, bf_ref, sel_ref, const_ref, qa_ref, ka_ref, carry_ref):
    t = pl.program_id(1)

    @pl.when(t == 0)
    def _():
        carry_ref[...] = jnp.zeros_like(carry_ref)

    x = f_ref[...] + bf_ref[...]
    log_f = -(jnp.maximum(-x, 0.0) + jnp.log1p(jnp.exp(-jnp.abs(x))))
    ts = x.shape[0]
    row = lax.broadcasted_iota(jnp.int32, (ts, ts), 0)
    col = lax.broadcasted_iota(jnp.int32, (ts, ts), 1)
    tri = (row >= col).astype(BF16)
    parts = jnp.dot(tri, jnp.concatenate(_split3(log_f), axis=1), preferred_element_type=F32)
    cum = (parts[:, :LANES] + parts[:, LANES:2 * LANES]) + parts[:, 2 * LANES:] + carry_ref[...]
    carry_ref[...] = cum[ts - 1:ts, :]
    pieces = jnp.concatenate(_split3(cum * LOG2E), axis=1)
    aug = jnp.dot(pieces, sel_ref[...], preferred_element_type=F32) + const_ref[...]
    qa_ref[...] = aug[:, :LANES].T.astype(BF16)
    ka_ref[...] = aug[:, LANES:].astype(BF16)


def _bias_selectors():
    sel = np.zeros((3 * LANES, 2 * LANES), np.float32)
    const = np.zeros((1, 2 * LANES), np.float32)
    for head in range(HEADS_B):
        base = BIAS_LANES * head
        for piece in range(3):
            sel[piece * LANES + head, base + piece] = 1.0
            const[0, base + 3 + piece] = 1.0
            const[0, LANES + base + piece] = 1.0
            sel[piece * LANES + head, LANES + base + 3 + piece] = -1.0
    return jnp.asarray(sel, BF16), jnp.asarray(const)


def _forget_bias(f, b_f_pad, seq):
    m = f.shape[0]
    ts = 512
    per_batch = seq // ts
    sel, const = _bias_selectors()
    fixed = lambda shape: pl.BlockSpec(shape, lambda b, t: (0, 0))
    return pl.pallas_call(
        _forget_kernel,
        grid=(m // seq, per_batch),
        in_specs=[
            pl.BlockSpec((ts, LANES), lambda b, t: (b * per_batch + t, 0)),
            fixed((1, LANES)), fixed(sel.shape), fixed(const.shape),
        ],
        out_specs=[
            pl.BlockSpec((LANES, ts), lambda b, t: (0, b * per_batch + t)),
            pl.BlockSpec((ts, LANES), lambda b, t: (b * per_batch + t, 0)),
        ],
        out_shape=[
            jax.ShapeDtypeStruct((LANES, m), BF16),
            jax.ShapeDtypeStruct((m, LANES), BF16),
        ],
        scratch_shapes=[pltpu.VMEM((1, LANES), F32)],
        name="forget_bias",
        compiler_params=pltpu.CompilerParams(
            dimension_semantics=("parallel", "arbitrary"), vmem_limit_bytes=VMEM_LIMIT),
    )(f, b_f_pad, sel, const)


def _spatial_gating(u_ref, v_ref, ga_ref, lng_ref, lnb_ref, ws_ref, bst_ref, o_ref):
    rows = u_ref.shape[0]
    v = v_ref[...].astype(F32)
    mu = jnp.mean(v, axis=-1, keepdims=True)
    vc = v - mu
    var = jnp.mean(vc * vc, axis=-1, keepdims=True)
    vn = (vc * lax.rsqrt(var + EPS) * lng_ref[...] + lnb_ref[...]).astype(BF16)
    trow = lax.broadcasted_iota(jnp.int32, (CHUNK, CHUNK), 0)
    tcol = lax.broadcasted_iota(jnp.int32, (CHUNK, CHUNK), 1)
    causal = trow >= tcol
    for h in range(HEADS_A):
        w = jnp.where(causal, ws_ref[h], 0.0).astype(BF16)
        bias = bst_ref[:, h:h + 1]
        cols = slice(h * DH_A, (h + 1) * DH_A)
        for c in range(rows // CHUNK):
            rws = slice(c * CHUNK, (c + 1) * CHUNK)
            mixed = jnp.dot(w, vn[rws, cols], preferred_element_type=F32) + bias
            y = u_ref[rws, cols].astype(F32) * mixed * ga_ref[rws, cols].astype(F32)
            o_ref[rws, cols] = y.astype(BF16)


def _attn_kernel(q_ref, k_ref, v_ref, gb_ref, qa_ref, ka_ref, o_ref,
                 kk_ref, vt_ref, qt_ref, s0_ref, s1_ref, mx0_ref, mx1_ref):
    pair = pl.program_id(1)
    seq = q_ref.shape[0]
    n_col = seq // COL_TILE

    def tile(c):
        return slice(c * COL_TILE, (c + 1) * COL_TILE)

    kk_ref[:, :LANES] = k_ref[...]
    kk_ref[:, LANES:] = ka_ref[...]
    ones = jnp.ones((ONES_ROWS, KEY_CHUNK), BF16)
    for j in range(seq // KEY_CHUNK):
        rows = slice(j * KEY_CHUNK, (j + 1) * KEY_CHUNK)
        vt = v_ref[rows, :].T
        vt_ref[0, :, rows] = jnp.concatenate([vt[:DH_B], ones], axis=0)
        vt_ref[1, :, rows] = jnp.concatenate([vt[DH_B:], ones], axis=0)

    def prepare(c):
        q_t = q_ref[tile(c), :].T
        qa_t = qa_ref[:, tile(c)].astype(F32)
        bias_row = lax.broadcasted_iota(jnp.int32, qa_t.shape, 0)
        zeros = jnp.zeros((DH_B, COL_TILE), BF16)
        for h in range(2):
            first = BIAS_LANES * (2 * pair + h)
            own = (bias_row >= first) & (bias_row < first + BIAS_LANES)
            bias_t = jnp.where(own, qa_t, 0.0).astype(BF16)
            q_h = q_t[:DH_B] if h == 0 else q_t[DH_B:]
            head_rows = [q_h, zeros] if h == 0 else [zeros, q_h]
            qt_ref[h, :, tile(c)] = jnp.concatenate(head_rows + [bias_t], axis=0)

    def causal(square):
        key = lax.broadcasted_iota(jnp.int32, square.shape, 0)
        qry = lax.broadcasted_iota(jnp.int32, square.shape, 1)
        return jnp.where(key <= qry, square, -jnp.inf)

    def emit(c, pv):
        o_t = jnp.concatenate([x[:DH_B] / x[DH_B:DH_B + 1] for x in pv], axis=0)
        o_ref[tile(c), :] = (o_t.T * gb_ref[tile(c), :].astype(F32)).astype(BF16)

    def scores(c, buf):
        s_ref, mx_ref = buf
        below = c * COL_TILE
        for h in range(2):
            rhs = qt_ref[h, :, tile(c)]
            square = causal(jnp.dot(kk_ref[tile(c), :], rhs, preferred_element_type=F32))
            s_ref[h, tile(c), :] = square
            mx = jnp.max(square, axis=0, keepdims=True)
            if below:
                full = jnp.dot(kk_ref[:below, :], rhs, preferred_element_type=F32)
                s_ref[h, :below, :] = full
                mx = jnp.maximum(mx, jnp.max(full, axis=0, keepdims=True))
            mx_ref[h] = mx

    origin = pl.multiple_of(jnp.minimum(pl.program_id(0), 0) * COL_TILE, COL_TILE)

    def softmax_pv(c, buf):
        s_ref, mx_ref = buf
        n_keys = (c + 1) * COL_TILE
        pv = []
        for h in range(2):
            p_t = jnp.exp2(s_ref[h, pl.ds(origin, n_keys), :] - mx_ref[h]).astype(BF16)
            pv.append(jnp.dot(vt_ref[h, :, :n_keys], p_t, preferred_element_type=F32))
        emit(c, pv)

    order = tuple(reversed(range(n_col)))
    for c in order:
        prepare(c)

    bufs = ((s0_ref, mx0_ref), (s1_ref, mx1_ref))
    scores(order[0], bufs[0])
    for n, c in enumerate(order):
        if n + 1 < n_col:
            scores(order[n + 1], bufs[(n + 1) % 2])
        softmax_pv(c, bufs[n % 2])


def _attention(z, qaug_t, kaug, batch, seq):
    m = z.shape[0]
    blk = WIDTH_B // LANES
    zblock = lambda col: pl.BlockSpec((seq, LANES), lambda b, p: (b, col * blk + p))
    return pl.pallas_call(
        _attn_kernel,
        grid=(batch, HEAD_PAIRS),
        in_specs=[
            zblock(COL_Q), zblock(COL_K), zblock(COL_VB), zblock(COL_GB),
            pl.BlockSpec((LANES, seq), lambda b, p: (0, b)),
            pl.BlockSpec((seq, LANES), lambda b, p: (b, 0)),
        ],
        out_specs=pl.BlockSpec((seq, LANES), lambda b, p: (b, p)),
        out_shape=jax.ShapeDtypeStruct((m, WIDTH_B), BF16),
        scratch_shapes=[
            pltpu.VMEM((seq, 2 * LANES), BF16),
            pltpu.VMEM((2, ACC_ROWS, seq), BF16),
            pltpu.VMEM((2, 2 * LANES, seq), BF16),
            pltpu.VMEM((2, seq, COL_TILE), F32),
            pltpu.VMEM((2, seq, COL_TILE), F32),
            pltpu.VMEM((2, 1, COL_TILE), F32),
            pltpu.VMEM((2, 1, COL_TILE), F32),
        ],
        name="fox_attention",
        compiler_params=pltpu.CompilerParams(
            dimension_semantics=("parallel", "parallel"), vmem_limit_bytes=VMEM_LIMIT),
    )(z, z, z, z, qaug_t, kaug)


def _outproj_kernel(u_ref, v_ref, ga_ref, lng_ref, lnb_ref, ws_ref, bst_ref, yb_ref, x_ref,
                    gate_ref, wa_ref, wb_ref, fg_ref, o_ref, ya_ref, *, final):
    _spatial_gating(u_ref, v_ref, ga_ref, lng_ref, lnb_ref, ws_ref, bst_ref, ya_ref)
    y = jnp.dot(ya_ref[...], wa_ref[...], preferred_element_type=F32)
    y = y + jnp.dot(yb_ref[...], wb_ref[...], preferred_element_type=F32)
    x = x_ref[...] + gate_ref[...] * y
    if final:
        x = x * lax.rsqrt(jnp.mean(x * x, axis=-1, keepdims=True) + EPS) * fg_ref[...]
    o_ref[...] = x


def _outproj(z, sgu_params, yb, x2, gate, w_out, final_g, layer, seq, final):
    m, d = x2.shape
    tm = 512
    per_batch = seq // tm
    zcol = lambda col: pl.BlockSpec((tm, WIDTH_A), lambda i: (i, col))
    fixed = lambda shape: pl.BlockSpec(shape, lambda i: (0,) * len(shape))
    return pl.pallas_call(
        functools.partial(_outproj_kernel, final=final),
        grid=(m // tm,),
        in_specs=[
            zcol(COL_U), zcol(COL_V), zcol(COL_GA),
            fixed((1, WIDTH_A)), fixed((1, WIDTH_A)),
            fixed((HEADS_A, CHUNK, CHUNK)), fixed((CHUNK, HEADS_A)),
            pl.BlockSpec((tm, WIDTH_B), lambda i: (i, 0)),
            pl.BlockSpec((tm, d), lambda i: (i, 0)),
            pl.BlockSpec((None, 1, d), lambda i: (i // per_batch, 0, 0)),
            pl.BlockSpec((None, WIDTH_A, d), lambda i: (layer, 0, 0)),
            pl.BlockSpec((None, WIDTH_B, d), lambda i: (layer, 1, 0)),
            fixed((1, d)),
        ],
        out_specs=pl.BlockSpec((tm, d), lambda i: (i, 0)),
        out_shape=jax.ShapeDtypeStruct((m, d), F32),
        scratch_shapes=[pltpu.VMEM((tm, WIDTH_A), BF16)],
        name="outproj_final" if final else "outproj",
        compiler_params=pltpu.CompilerParams(
            dimension_semantics=("parallel",), vmem_limit_bytes=VMEM_LIMIT),
    )(z, z, z, *sgu_params, yb, x2, gate, w_out, w_out, final_g)


def kernel(x, c, norm_g, w_ada, b_ada, w_in, ln_v_g, ln_v_b, w_s, b_s, b_f, w_out, final_g):
    batch, seq, d = x.shape
    depth = w_in.shape[0]
    c_pad = jnp.pad(c, ((0, 8 - batch), (0, 0)))
    mod = _ada_modulation(c_pad, w_ada, b_ada)[:, :batch]
    mod = mod.reshape(depth, batch, 3, 1, d)
    w_in_bf = w_in.astype(BF16)
    w_out_bf = w_out.astype(BF16)
    x2 = x.reshape(batch * seq, d)
    for l in range(depth):
        shift, scale, gate = mod[l, :, 0], mod[l, :, 1], mod[l, :, 2]
        w_f = jnp.pad(w_in_bf[l, :, N_MAIN:], ((0, 0), (0, LANES - HEADS_B)))
        z, f = _inproj(x2, shift, scale, norm_g[l][None], w_in_bf, w_f, l, seq)
        b_f_pad = jnp.pad(b_f[l], (0, LANES - HEADS_B))[None]
        qaug_t, kaug = _forget_bias(f, b_f_pad, seq)
        yb = _attention(z, qaug_t, kaug, batch, seq)
        sgu_params = (ln_v_g[l][None], ln_v_b[l][None], w_s[l], b_s[l].T)
        x2 = _outproj(z, sgu_params, yb, x2, gate, w_out_bf, final_g[None], l, seq,
                      final=(l == depth - 1))
    return x2.reshape(batch, seq, d)
```

```python
import functools

import jax
import jax.numpy as jnp
import numpy as np
from jax import lax
from jax.experimental import pallas as pl
from jax.experimental.pallas import tpu as pltpu

D_MODEL = 2048
WIDTH_A = 1024
WIDTH_B = 1024
CHUNK = 128
HEADS_A = 8
DH_A = WIDTH_A // HEADS_A
HEADS_B = 16
DH_B = WIDTH_B // HEADS_B
EPS = 1e-6

LANES = 128
HEAD_PAIRS = HEADS_B // 2
BIAS_LANES = 6
N_MAIN = 3 * WIDTH_A + 4 * WIDTH_B
VMEM_LIMIT = 56 * 1024 * 1024

PROJ_SUB = 256
PROJ_TILE = 7 * PROJ_SUB
COL_TILE = 256
KEY_CHUNK = 1024
ONES_ROWS = 16
ACC_ROWS = DH_B + ONES_ROWS

LOG2E = np.float32(1.0 / np.log(2.0))

COL_U, COL_V, COL_GA, COL_Q, COL_K, COL_VB, COL_GB = range(7)

BF16 = jnp.bfloat16
F32 = jnp.float32


def _silu(x):
    return x / (1.0 + jnp.exp(-x))


def _gelu(x):
    return 0.5 * x * (1.0 + lax.erf(x * np.float32(1.0 / np.sqrt(2.0))))


def _ada_kernel(c_ref, w_ref, b_ref, o_ref):
    cond = _silu(c_ref[...]).astype(BF16)
    o_ref[...] = jnp.dot(cond, w_ref[...].astype(BF16), preferred_element_type=F32) + b_ref[...]


def _ada_modulation(c_pad, w_ada, b_ada):
    depth, d, n = w_ada.shape
    tn = 1536
    rows = c_pad.shape[0]
    return pl.pallas_call(
        _ada_kernel,
        grid=(depth, n // tn),
        in_specs=[
            pl.BlockSpec((rows, d), lambda l, j: (0, 0)),
            pl.BlockSpec((None, d, tn), lambda l, j: (l, 0, j)),
            pl.BlockSpec((None, 1, tn), lambda l, j: (l, 0, j)),
        ],
        out_specs=pl.BlockSpec((None, rows, tn), lambda l, j: (l, 0, j)),
        out_shape=jax.ShapeDtypeStruct((depth, rows, n), F32),
        name="ada_modulation",
        compiler_params=pltpu.CompilerParams(
            dimension_semantics=("arbitrary", "arbitrary"), vmem_limit_bytes=VMEM_LIMIT),
    )(c_pad, w_ada, b_ada.reshape(depth, 1, n))


def _inproj_kernel(x_ref, shift_ref, scale_ref, g_ref, w_ref, wf_ref, z_ref, f_ref, h_ref):
    j = pl.program_id(1)

    def normalize():
        x = x_ref[...]
        y = x * lax.rsqrt(jnp.mean(x * x, axis=-1, keepdims=True) + EPS) * g_ref[...]
        h = (y * (1.0 + scale_ref[...]) + shift_ref[...]).astype(BF16)
        h_ref[...] = h
        f_ref[...] = jnp.dot(h, wf_ref[...], preferred_element_type=F32)

    plain = lambda acc: acc
    group_act = (_gelu, _gelu, _silu, lambda acc: acc * (DH_B ** -0.5 * LOG2E), plain, plain, _silu)

    def project(tile_index):
        for c in range(PROJ_TILE // PROJ_SUB):
            act = group_act[(tile_index * PROJ_TILE + c * PROJ_SUB) // WIDTH_A]
            cols = slice(c * PROJ_SUB, (c + 1) * PROJ_SUB)
            acc = jnp.dot(h_ref[...], w_ref[:, cols], preferred_element_type=F32)
            z_ref[:, cols] = act(acc).astype(BF16)

    for tile_index in range(N_MAIN // PROJ_TILE):
        @pl.when(j == tile_index)
        def _(tile_index=tile_index):
            if tile_index == 0:
                normalize()
            project(tile_index)


def _inproj(x2, shift, scale, g, w_in, w_f, layer, seq):
    m, d = x2.shape
    tm, tn = 1024, PROJ_TILE
    per_batch = seq // tm
    return pl.pallas_call(
        _inproj_kernel,
        grid=(m // tm, N_MAIN // tn),
        in_specs=[
            pl.BlockSpec((tm, d), lambda i, j: (i, 0)),
            pl.BlockSpec((None, 1, d), lambda i, j: (i // per_batch, 0, 0)),
            pl.BlockSpec((None, 1, d), lambda i, j: (i // per_batch, 0, 0)),
            pl.BlockSpec((1, d), lambda i, j: (0, 0)),
            pl.BlockSpec((None, d, tn), lambda i, j: (layer, 0, j)),
            pl.BlockSpec((d, LANES), lambda i, j: (0, 0)),
        ],
        out_specs=[
            pl.BlockSpec((tm, tn), lambda i, j: (i, j)),
            pl.BlockSpec((tm, LANES), lambda i, j: (i, 0)),
        ],
        out_shape=[
            jax.ShapeDtypeStruct((m, N_MAIN), BF16),
            jax.ShapeDtypeStruct((m, LANES), F32),
        ],
        scratch_shapes=[pltpu.VMEM((tm, d), BF16)],
        name="norm_inproj",
        compiler_params=pltpu.CompilerParams(
            dimension_semantics=("parallel", "arbitrary"), vmem_limit_bytes=VMEM_LIMIT),
    )(x2, shift, scale, g, w_in, w_f)


def _split3(x):
    hi = x.astype(BF16)
    r1 = x - hi.astype(F32)
    mid = r1.astype(BF16)
    lo = (r1 - mid.astype(F32)).astype(BF16)
    return hi, mid, lo


def _forget_kernel(f_ref, bf_ref, sel_ref, const_ref, qa_ref, ka_ref, carry_ref):
    t = pl.program_id(1)

    @pl.when(t == 0)
    def _():
        carry_ref[...] = jnp.zeros_like(carry_ref)

    x = f_ref[...] + bf_ref[...]
    log_f = -(jnp.maximum(-x, 0.0) + jnp.log1p(jnp.exp(-jnp.abs(x))))
    ts = x.shape[0]
    row = lax.broadcasted_iota(jnp.int32, (ts, ts), 0)
    col = lax.broadcasted_iota(jnp.int32, (ts, ts), 1)
    tri = (row >= col).astype(BF16)
    parts = jnp.dot(tri, jnp.concatenate(_split3(log_f), axis=1), preferred_element_type=F32)
    cum = (parts[:, :LANES] + parts[:, LANES:2 * LANES]) + parts[:, 2 * LANES:] + carry_ref[...]
    carry_ref[...] = cum[ts - 1:ts, :]
    pieces = jnp.concatenate(_split3(cum * LOG2E), axis=1)
    aug = jnp.dot(pieces, sel_ref[...], preferred_element_type=F32) + const_ref[...]
    qa_ref[...] = aug[:, :LANES].T.astype(BF16)
    ka_ref[...] = aug[:, LANES:].astype(BF16)


def _bias_selectors():
    sel = np.zeros((3 * LANES, 2 * LANES), np.float32)
    const = np.zeros((1, 2 * LANES), np.float32)
    for head in range(HEADS_B):
        base = BIAS_LANES * head
        for piece in range(3):
            sel[piece * LANES + head, base + piece] = 1.0
            const[0, base + 3 + piece] = 1.0
            const[0, LANES + base + piece] = 1.0
            sel[piece * LANES + head, LANES + base + 3 + piece] = -1.0
    return jnp.asarray(sel, BF16), jnp.asarray(const)


def _forget_bias(f, b_f_pad, seq):
    m = f.shape[0]
    ts = 512
    per_batch = seq // ts
    sel, const = _bias_selectors()
    fixed = lambda shape: pl.BlockSpec(shape, lambda b, t: (0, 0))
    return pl.pallas_call(
        _forget_kernel,
        grid=(m // seq, per_batch),
        in_specs=[
            pl.BlockSpec((ts, LANES), lambda b, t: (b * per_batch + t, 0)),
            fixed((1, LANES)), fixed(sel.shape), fixed(const.shape),
        ],
        out_specs=[
            pl.BlockSpec((LANES, ts), lambda b, t: (0, b * per_batch + t)),
            pl.BlockSpec((ts, LANES), lambda b, t: (b * per_batch + t, 0)),
        ],
        out_shape=[
            jax.ShapeDtypeStruct((LANES, m), BF16),
            jax.ShapeDtypeStruct((m, LANES), BF16),
        ],
        scratch_shapes=[pltpu.VMEM((1, LANES), F32)],
        name="forget_bias",
        compiler_params=pltpu.CompilerParams(
            dimension_semantics=("parallel", "arbitrary"), vmem_limit_bytes=VMEM_LIMIT),
    )(f, b_f_pad, sel, const)


def _spatial_gating(u_ref, v_ref, ga_ref, lng_ref, lnb_ref, ws_ref, bst_ref, o_ref):
    rows = u_ref.shape[0]
    v = v_ref[...].astype(F32)
    mu = jnp.mean(v, axis=-1, keepdims=True)
    vc = v - mu
    var = jnp.mean(vc * vc, axis=-1, keepdims=True)
    vn = (vc * lax.rsqrt(var + EPS) * lng_ref[...] + lnb_ref[...]).astype(BF16)
    trow = lax.broadcasted_iota(jnp.int32, (CHUNK, CHUNK), 0)
    tcol = lax.broadcasted_iota(jnp.int32, (CHUNK, CHUNK), 1)
    causal = trow >= tcol
    for h in range(HEADS_A):
        w = jnp.where(causal, ws_ref[h], 0.0).astype(BF16)
        bias = bst_ref[:, h:h + 1]
        cols = slice(h * DH_A, (h + 1) * DH_A)
        for c in range(rows // CHUNK):
            rws = slice(c * CHUNK, (c + 1) * CHUNK)
            mixed = jnp.dot(w, vn[rws, cols], preferred_element_type=F32) + bias
            y = u_ref[rws, cols].astype(F32) * mixed * ga_ref[rws, cols].astype(F32)
            o_ref[rws, cols] = y.astype(BF16)


def _attn_kernel(q_ref, k_ref, v_ref, gb_ref, qa_ref, ka_ref, o_ref,
                 kk_ref, vt_ref, qt_ref, s0_ref, s1_ref, mx0_ref, mx1_ref):
    pair = pl.program_id(1)
    seq = q_ref.shape[0]
    n_col = seq // COL_TILE

    def tile(c):
        return slice(c * COL_TILE, (c + 1) * COL_TILE)

    kk_ref[:, :LANES] = k_ref[...]
    kk_ref[:, LANES:] = ka_ref[...]
    ones = jnp.ones((ONES_ROWS, KEY_CHUNK), BF16)
    for j in range(seq // KEY_CHUNK):
        rows = slice(j * KEY_CHUNK, (j + 1) * KEY_CHUNK)
        vt = v_ref[rows, :].T
        vt_ref[0, :, rows] = jnp.concatenate([vt[:DH_B], ones], axis=0)
        vt_ref[1, :, rows] = jnp.concatenate([vt[DH_B:], ones], axis=0)

    def prepare(c):
        q_t = q_ref[tile(c), :].T
        qa_t = qa_ref[:, tile(c)].astype(F32)
        bias_row = lax.broadcasted_iota(jnp.int32, qa_t.shape, 0)
        zeros = jnp.zeros((DH_B, COL_TILE), BF16)
        for h in range(2):
            first = BIAS_LANES * (2 * pair + h)
            own = (bias_row >= first) & (bias_row < first + BIAS_LANES)
            bias_t = jnp.where(own, qa_t, 0.0).astype(BF16)
            q_h = q_t[:DH_B] if h == 0 else q_t[DH_B:]
            head_rows = [q_h, zeros] if h == 0 else [zeros, q_h]
            qt_ref[h, :, tile(c)] = jnp.concatenate(head_rows + [bias_t], axis=0)

    def causal(square):
        key = lax.broadcasted_iota(jnp.int32, square.shape, 0)
        qry = lax.broadcasted_iota(jnp.int32, square.shape, 1)
        return jnp.where(key <= qry, square, -jnp.inf)

    def emit(c, pv):
        o_t = jnp.concatenate([x[:DH_B] / x[DH_B:DH_B + 1] for x in pv], axis=0)
        o_ref[tile(c), :] = (o_t.T * gb_ref[tile(c), :].astype(F32)).astype(BF16)

    def scores(c, buf):
        s_ref, mx_ref = buf
        below = c * COL_TILE
        for h in range(2):
            rhs = qt_ref[h, :, tile(c)]
            square = causal(jnp.dot(kk_ref[tile(c), :], rhs, preferred_element_type=F32))
            s_ref[h, tile(c), :] = square
            mx = jnp.max(square, axis=0, keepdims=True)
            if below:
                full = jnp.dot(kk_ref[:below, :], rhs, preferred_element_type=F32)
                s_ref[h, :below, :] = full
                mx = jnp.maximum(mx, jnp.max(full, axis=0, keepdims=True))
            mx_ref[h] = mx

    origin = pl.multiple_of(jnp.minimum(pl.program_id(0), 0) * COL_TILE, COL_TILE)

    def softmax_pv(c, buf):
        s_ref, mx_ref = buf
        n_keys = (c + 1) * COL_TILE
        pv = []
        for h in range(2):
            p_t = jnp.exp2(s_ref[h, pl.ds(origin, n_keys), :] - mx_ref[h]).astype(BF16)
            pv.append(jnp.dot(vt_ref[h, :, :n_keys], p_t, preferred_element_type=F32))
        emit(c, pv)

    order = tuple(reversed(range(n_col)))
    for c in order:
        prepare(c)

    bufs = ((s0_ref, mx0_ref), (s1_ref, mx1_ref))
    scores(order[0], bufs[0])
    for n, c in enumerate(order):
        if n + 1 < n_col:
            scores(order[n + 1], bufs[(n + 1) % 2])
        softmax_pv(c, bufs[n % 2])


def _attention(z, qaug_t, kaug, batch, seq):
    m = z.shape[0]
    blk = WIDTH_B // LANES
    zblock = lambda col: pl.BlockSpec((seq, LANES), lambda b, p: (b, col * blk + p))
    return pl.pallas_call(
        _attn_kernel,
        grid=(batch, HEAD_PAIRS),
        in_specs=[
            zblock(COL_Q), zblock(COL_K), zblock(COL_VB), zblock(COL_GB),
            pl.BlockSpec((LANES, seq), lambda b, p: (0, b)),
            pl.BlockSpec((seq, LANES), lambda b, p: (b, 0)),
        ],
        out_specs=pl.BlockSpec((seq, LANES), lambda b, p: (b, p)),
        out_shape=jax.ShapeDtypeStruct((m, WIDTH_B), BF16),
        scratch_shapes=[
            pltpu.VMEM((seq, 2 * LANES), BF16),
            pltpu.VMEM((2, ACC_ROWS, seq), BF16),
            pltpu.VMEM((2, 2 * LANES, seq), BF16),
            pltpu.VMEM((2, seq, COL_TILE), F32),
            pltpu.VMEM((2, seq, COL_TILE), F32),
            pltpu.VMEM((2, 1, COL_TILE), F32),
            pltpu.VMEM((2, 1, COL_TILE), F32),
        ],
        name="fox_attention",
        compiler_params=pltpu.CompilerParams(
            dimension_semantics=("parallel", "parallel"), vmem_limit_bytes=VMEM_LIMIT),
    )(z, z, z, z, qaug_t, kaug)


def _outproj_kernel(u_ref, v_ref, ga_ref, lng_ref, lnb_ref, ws_ref, bst_ref, yb_ref, x_ref,
                    gate_ref, wa_ref, wb_ref, fg_ref, o_ref, ya_ref, *, final):
    x = x_ref[...] + gate_ref[...] * jnp.dot(yb_ref[...], wb_ref[...], preferred_element_type=F32)
    _spatial_gating(u_ref, v_ref, ga_ref, lng_ref, lnb_ref, ws_ref, bst_ref, ya_ref)
    x = x + gate_ref[...] * jnp.dot(ya_ref[...], wa_ref[...], preferred_element_type=F32)
    if final:
        x = x * lax.rsqrt(jnp.mean(x * x, axis=-1, keepdims=True) + EPS) * fg_ref[...]
    o_ref[...] = x


def _outproj(z, sgu_params, yb, x2, gate, w_out, final_g, layer, seq, final):
    m, d = x2.shape
    tm = 512
    per_batch = seq // tm
    zcol = lambda col: pl.BlockSpec((tm, WIDTH_A), lambda i: (i, col))
    fixed = lambda shape: pl.BlockSpec(shape, lambda i: (0,) * len(shape))
    return pl.pallas_call(
        functools.partial(_outproj_kernel, final=final),
        grid=(m // tm,),
        in_specs=[
            zcol(COL_U), zcol(COL_V), zcol(COL_GA),
            fixed((1, WIDTH_A)), fixed((1, WIDTH_A)),
            fixed((HEADS_A, CHUNK, CHUNK)), fixed((CHUNK, HEADS_A)),
            pl.BlockSpec((tm, WIDTH_B), lambda i: (i, 0)),
            pl.BlockSpec((tm, d), lambda i: (i, 0)),
            pl.BlockSpec((None, 1, d), lambda i: (i // per_batch, 0, 0)),
            pl.BlockSpec((None, WIDTH_A, d), lambda i: (layer, 0, 0)),
            pl.BlockSpec((None, WIDTH_B, d), lambda i: (layer, 1, 0)),
            fixed((1, d)),
        ],
        out_specs=pl.BlockSpec((tm, d), lambda i: (i, 0)),
        out_shape=jax.ShapeDtypeStruct((m, d), F32),
        scratch_shapes=[pltpu.VMEM((tm, WIDTH_A), BF16)],
        name="outproj_final" if final else "outproj",
        compiler_params=pltpu.CompilerParams(
            dimension_semantics=("parallel",), vmem_limit_bytes=VMEM_LIMIT),
    )(z, z, z, *sgu_params, yb, x2, gate, w_out, w_out, final_g)


def kernel(x, c, norm_g, w_ada, b_ada, w_in, ln_v_g, ln_v_b, w_s, b_s, b_f, w_out, final_g):
    batch, seq, d = x.shape
    depth = w_in.shape[0]
    c_pad = jnp.pad(c, ((0, 8 - batch), (0, 0)))
    mod = _ada_modulation(c_pad, w_ada, b_ada)[:, :batch]
    mod = mod.reshape(depth, batch, 3, 1, d)
    w_in_bf = w_in.astype(BF16)
    w_out_bf = w_out.astype(BF16)
    x2 = x.reshape(batch * seq, d)
    for l in range(depth):
        shift, scale, gate = mod[l, :, 0], mod[l, :, 1], mod[l, :, 2]
        w_f = jnp.pad(w_in_bf[l, :, N_MAIN:], ((0, 0), (0, LANES - HEADS_B)))
        z, f = _inproj(x2, shift, scale, norm_g[l][None], w_in_bf, w_f, l, seq)
        b_f_pad = jnp.pad(b_f[l], (0, LANES - HEADS_B))[None]
        qaug_t, kaug = _forget_bias(f, b_f_pad, seq)
        yb = _attention(z, qaug_t, kaug, batch, seq)
        sgu_params = (ln_v_g[l][None], ln_v_b[l][None], w_s[l], b_s[l].T)
        x2 = _outproj(z, sgu_params, yb, x2, gate, w_out_bf, final_g[None], l, seq,
                      final=(l == depth - 1))
    return x2.reshape(batch, seq, d)
```

```python
import functools

import jax
import jax.numpy as jnp
import numpy as np
from jax import lax
from jax.experimental import pallas as pl
from jax.experimental.pallas import tpu as pltpu

D_MODEL = 2048
WIDTH_A = 1024
WIDTH_B = 1024
CHUNK = 128
HEADS_A = 8
DH_A = WIDTH_A // HEADS_A
HEADS_B = 16
DH_B = WIDTH_B // HEADS_B
EPS = 1e-6

LANES = 128
HEAD_PAIRS = HEADS_B // 2
BIAS_LANES = 6
N_MAIN = 3 * WIDTH_A + 4 * WIDTH_B
VMEM_LIMIT = 56 * 1024 * 1024

PROJ_SUB = 256
PROJ_TILE = 7 * PROJ_SUB
COL_TILE = 256
KEY_CHUNK = 1024
KEY_STEP = 512
ONES_ROWS = 16
ACC_ROWS = DH_B + ONES_ROWS

LOG2E = np.float32(1.0 / np.log(2.0))

COL_U, COL_V, COL_GA, COL_Q, COL_K, COL_VB, COL_GB = range(7)

BF16 = jnp.bfloat16
F32 = jnp.float32


def _silu(x):
    return x / (1.0 + jnp.exp(-x))


def _gelu(x):
    return 0.5 * x * (1.0 + lax.erf(x * np.float32(1.0 / np.sqrt(2.0))))


def _ada_kernel(c_ref, w_ref, b_ref, o_ref):
    cond = _silu(c_ref[...]).astype(BF16)
    o_ref[...] = jnp.dot(cond, w_ref[...].astype(BF16), preferred_element_type=F32) + b_ref[...]


def _ada_modulation(c_pad, w_ada, b_ada):
    depth, d, n = w_ada.shape
    tn = 1536
    rows = c_pad.shape[0]
    return pl.pallas_call(
        _ada_kernel,
        grid=(depth, n // tn),
        in_specs=[
            pl.BlockSpec((rows, d), lambda l, j: (0, 0)),
            pl.BlockSpec((None, d, tn), lambda l, j: (l, 0, j)),
            pl.BlockSpec((None, 1, tn), lambda l, j: (l, 0, j)),
        ],
        out_specs=pl.BlockSpec((None, rows, tn), lambda l, j: (l, 0, j)),
        out_shape=jax.ShapeDtypeStruct((depth, rows, n), F32),
        name="ada_modulation",
        compiler_params=pltpu.CompilerParams(
            dimension_semantics=("arbitrary", "arbitrary"), vmem_limit_bytes=VMEM_LIMIT),
    )(c_pad, w_ada, b_ada.reshape(depth, 1, n))


def _inproj_kernel(x_ref, shift_ref, scale_ref, g_ref, w_ref, wf_ref, z_ref, f_ref, h_ref):
    j = pl.program_id(1)

    def normalize():
        x = x_ref[...]
        y = x * lax.rsqrt(jnp.mean(x * x, axis=-1, keepdims=True) + EPS) * g_ref[...]
        h = (y * (1.0 + scale_ref[...]) + shift_ref[...]).astype(BF16)
        h_ref[...] = h
        f_ref[...] = jnp.dot(h, wf_ref[...], preferred_element_type=F32)

    plain = lambda acc: acc
    group_act = (_gelu, _gelu, _silu, lambda acc: acc * (DH_B ** -0.5 * LOG2E), plain, plain, _silu)

    def project(tile_index):
        for c in range(PROJ_TILE // PROJ_SUB):
            act = group_act[(tile_index * PROJ_TILE + c * PROJ_SUB) // WIDTH_A]
            cols = slice(c * PROJ_SUB, (c + 1) * PROJ_SUB)
            acc = jnp.dot(h_ref[...], w_ref[:, cols], preferred_element_type=F32)
            z_ref[:, cols] = act(acc).astype(BF16)

    for tile_index in range(N_MAIN // PROJ_TILE):
        @pl.when(j == tile_index)
        def _(tile_index=tile_index):
            if tile_index == 0:
                normalize()
            project(tile_index)


def _inproj(x2, shift, scale, g, w_in, w_f, layer, seq):
    m, d = x2.shape
    tm, tn = 1024, PROJ_TILE
    per_batch = seq // tm
    return pl.pallas_call(
        _inproj_kernel,
        grid=(m // tm, N_MAIN // tn),
        in_specs=[
            pl.BlockSpec((tm, d), lambda i, j: (i, 0)),
            pl.BlockSpec((None, 1, d), lambda i, j: (i // per_batch, 0, 0)),
            pl.BlockSpec((None, 1, d), lambda i, j: (i // per_batch, 0, 0)),
            pl.BlockSpec((1, d), lambda i, j: (0, 0)),
            pl.BlockSpec((None, d, tn), lambda i, j: (layer, 0, j)),
            pl.BlockSpec((d, LANES), lambda i, j: (0, 0)),
        ],
        out_specs=[
            pl.BlockSpec((tm, tn), lambda i, j: (i, j)),
            pl.BlockSpec((tm, LANES), lambda i, j: (i, 0)),
        ],
        out_shape=[
            jax.ShapeDtypeStruct((m, N_MAIN), BF16),
            jax.ShapeDtypeStruct((m, LANES), F32),
        ],
        scratch_shapes=[pltpu.VMEM((tm, d), BF16)],
        name="norm_inproj",
        compiler_params=pltpu.CompilerParams(
            dimension_semantics=("parallel", "arbitrary"), vmem_limit_bytes=VMEM_LIMIT),
    )(x2, shift, scale, g, w_in, w_f)


def _split3(x):
    hi = x.astype(BF16)
    r1 = x - hi.astype(F32)
    mid = r1.astype(BF16)
    lo = (r1 - mid.astype(F32)).astype(BF16)
    return hi, mid, lo


def _forget_kernel(f_ref, bf_ref, sel_ref, const_ref, qa_ref, ka_ref, carry_ref):
    t = pl.program_id(1)

    @pl.when(t == 0)
    def _():
        carry_ref[...] = jnp.zeros_like(carry_ref)

    x = f_ref[...] + bf_ref[...]
    log_f = -(jnp.maximum(-x, 0.0) + jnp.log1p(jnp.exp(-jnp.abs(x))))
    ts = x.shape[0]
    row = lax.broadcasted_iota(jnp.int32, (ts, ts), 0)
    col = lax.broadcasted_iota(jnp.int32, (ts, ts), 1)
    tri = (row >= col).astype(BF16)
    parts = jnp.dot(tri, jnp.concatenate(_split3(log_f), axis=1), preferred_element_type=F32)
    cum = (parts[:, :LANES] + parts[:, LANES:2 * LANES]) + parts[:, 2 * LANES:] + carry_ref[...]
    carry_ref[...] = cum[ts - 1:ts, :]
    pieces = jnp.concatenate(_split3(cum * LOG2E), axis=1)
    aug = jnp.dot(pieces, sel_ref[...], preferred_element_type=F32) + const_ref[...]
    qa_ref[...] = aug[:, :LANES].T.astype(BF16)
    ka_ref[...] = aug[:, LANES:].astype(BF16)


def _bias_selectors():
    sel = np.zeros((3 * LANES, 2 * LANES), np.float32)
    const = np.zeros((1, 2 * LANES), np.float32)
    for head in range(HEADS_B):
        base = BIAS_LANES * head
        for piece in range(3):
            sel[piece * LANES + head, base + piece] = 1.0
            const[0, base + 3 + piece] = 1.0
            const[0, LANES + base + piece] = 1.0
            sel[piece * LANES + head, LANES + base + 3 + piece] = -1.0
    return jnp.asarray(sel, BF16), jnp.asarray(const)


def _forget_bias(f, b_f_pad, seq):
    m = f.shape[0]
    ts = 512
    per_batch = seq // ts
    sel, const = _bias_selectors()
    fixed = lambda shape: pl.BlockSpec(shape, lambda b, t: (0, 0))
    return pl.pallas_call(
        _forget_kernel,
        grid=(m // seq, per_batch),
        in_specs=[
            pl.BlockSpec((ts, LANES), lambda b, t: (b * per_batch + t, 0)),
            fixed((1, LANES)), fixed(sel.shape), fixed(const.shape),
        ],
        out_specs=[
            pl.BlockSpec((LANES, ts), lambda b, t: (0, b * per_batch + t)),
            pl.BlockSpec((ts, LANES), lambda b, t: (b * per_batch + t, 0)),
        ],
        out_shape=[
            jax.ShapeDtypeStruct((LANES, m), BF16),
            jax.ShapeDtypeStruct((m, LANES), BF16),
        ],
        scratch_shapes=[pltpu.VMEM((1, LANES), F32)],
        name="forget_bias",
        compiler_params=pltpu.CompilerParams(
            dimension_semantics=("parallel", "arbitrary"), vmem_limit_bytes=VMEM_LIMIT),
    )(f, b_f_pad, sel, const)


def _layer_norm_rows(v_ref, lng_ref, lnb_ref, vn_ref, rows):
    v = v_ref[rows, :].astype(F32)
    mu = jnp.mean(v, axis=-1, keepdims=True)
    vc = v - mu
    var = jnp.mean(vc * vc, axis=-1, keepdims=True)
    vn_ref[rows, :] = (vc * lax.rsqrt(var + EPS) * lng_ref[...] + lnb_ref[...]).astype(BF16)


def _gate_head(h, u_ref, vn_ref, ga_ref, ws_ref, bst_ref, o_ref):
    trow = lax.broadcasted_iota(jnp.int32, (CHUNK, CHUNK), 0)
    tcol = lax.broadcasted_iota(jnp.int32, (CHUNK, CHUNK), 1)
    w = jnp.where(trow >= tcol, ws_ref[h], 0.0).astype(BF16)
    bias = bst_ref[:, h:h + 1]
    cols = slice(h * DH_A, (h + 1) * DH_A)
    for c in range(u_ref.shape[0] // CHUNK):
        rws = slice(c * CHUNK, (c + 1) * CHUNK)
        mixed = jnp.dot(w, vn_ref[rws, cols], preferred_element_type=F32) + bias
        y = u_ref[rws, cols].astype(F32) * mixed * ga_ref[rws, cols].astype(F32)
        o_ref[rws, cols] = y.astype(BF16)


def _attn_kernel(q_ref, k_ref, v_ref, gb_ref, qa_ref, ka_ref, o_ref,
                 kk_ref, vt_ref, qt_ref, s0_ref, s1_ref, mx0_ref, mx1_ref):
    pair = pl.program_id(1)
    seq = q_ref.shape[0]
    n_col = seq // COL_TILE

    def tile(c):
        return slice(c * COL_TILE, (c + 1) * COL_TILE)

    kk_ref[:, :LANES] = k_ref[...]
    kk_ref[:, LANES:] = ka_ref[...]
    ones = jnp.ones((ONES_ROWS, KEY_CHUNK), BF16)
    for j in range(seq // KEY_CHUNK):
        rows = slice(j * KEY_CHUNK, (j + 1) * KEY_CHUNK)
        vt = v_ref[rows, :].T
        vt_ref[0, :, rows] = jnp.concatenate([vt[:DH_B], ones], axis=0)
        vt_ref[1, :, rows] = jnp.concatenate([vt[DH_B:], ones], axis=0)

    def prepare(c):
        q_t = q_ref[tile(c), :].T
        qa_t = qa_ref[:, tile(c)].astype(F32)
        bias_row = lax.broadcasted_iota(jnp.int32, qa_t.shape, 0)
        zeros = jnp.zeros((DH_B, COL_TILE), BF16)
        for h in range(2):
            first = BIAS_LANES * (2 * pair + h)
            own = (bias_row >= first) & (bias_row < first + BIAS_LANES)
            bias_t = jnp.where(own, qa_t, 0.0).astype(BF16)
            q_h = q_t[:DH_B] if h == 0 else q_t[DH_B:]
            head_rows = [q_h, zeros] if h == 0 else [zeros, q_h]
            qt_ref[h, :, tile(c)] = jnp.concatenate(head_rows + [bias_t], axis=0)

    def causal(square):
        key = lax.broadcasted_iota(jnp.int32, square.shape, 0)
        qry = lax.broadcasted_iota(jnp.int32, square.shape, 1)
        return jnp.where(key <= qry, square, -jnp.inf)

    def emit(c, pv):
        o_t = jnp.concatenate([x[:DH_B] / x[DH_B:DH_B + 1] for x in pv], axis=0)
        o_ref[tile(c), :] = (o_t.T * gb_ref[tile(c), :].astype(F32)).astype(BF16)

    def score_rows(c, lo, hi, buf, mx):
        s_ref, _ = buf
        diag = c * COL_TILE
        for h in range(2):
            rhs = qt_ref[h, :, tile(c)]
            if lo < diag:
                top = min(hi, diag)
                full = jnp.dot(kk_ref[lo:top, :], rhs, preferred_element_type=F32)
                s_ref[h, lo:top, :] = full
                m = jnp.max(full, axis=0, keepdims=True)
                mx[h] = m if mx[h] is None else jnp.maximum(mx[h], m)
            if hi > diag:
                square = causal(jnp.dot(kk_ref[tile(c), :], rhs, preferred_element_type=F32))
                s_ref[h, tile(c), :] = square
                m = jnp.max(square, axis=0, keepdims=True)
                mx[h] = m if mx[h] is None else jnp.maximum(mx[h], m)

    origin = pl.multiple_of(jnp.minimum(pl.program_id(0), 0) * COL_TILE, COL_TILE)

    def step(c, buf, buf_next):
        s_ref, mx_ref = buf
        n_keys = (c + 1) * COL_TILE
        n_next = c * COL_TILE
        pv = [None, None]
        mx_next = [None, None]
        for lo in range(0, n_keys, KEY_STEP):
            hi = min(lo + KEY_STEP, n_keys)
            if lo < n_next:
                score_rows(c - 1, lo, min(hi, n_next), buf_next, mx_next)
            for h in range(2):
                p_t = jnp.exp2(s_ref[h, pl.ds(origin + lo, hi - lo), :] - mx_ref[h]).astype(BF16)
                part = jnp.dot(vt_ref[h, :, lo:hi], p_t, preferred_element_type=F32)
                pv[h] = part if pv[h] is None else pv[h] + part
        if c > 0:
            for h in range(2):
                buf_next[1][h] = mx_next[h]
        emit(c, pv)

    order = tuple(reversed(range(n_col)))
    for c in order:
        prepare(c)

    bufs = ((s0_ref, mx0_ref), (s1_ref, mx1_ref))
    first = order[0]
    mx_first = [None, None]
    for lo in range(0, (first + 1) * COL_TILE, KEY_STEP):
        score_rows(first, lo, min(lo + KEY_STEP, (first + 1) * COL_TILE), bufs[0], mx_first)
    for h in range(2):
        mx0_ref[h] = mx_first[h]
    for n, c in enumerate(order):
        step(c, bufs[n % 2], bufs[(n + 1) % 2])


def _attention(z, qaug_t, kaug, batch, seq):
    m = z.shape[0]
    blk = WIDTH_B // LANES
    zblock = lambda col: pl.BlockSpec((seq, LANES), lambda b, p: (b, col * blk + p))
    return pl.pallas_call(
        _attn_kernel,
        grid=(batch, HEAD_PAIRS),
        in_specs=[
            zblock(COL_Q), zblock(COL_K), zblock(COL_VB), zblock(COL_GB),
            pl.BlockSpec((LANES, seq), lambda b, p: (0, b)),
            pl.BlockSpec((seq, LANES), lambda b, p: (b, 0)),
        ],
        out_specs=pl.BlockSpec((seq, LANES), lambda b, p: (b, p)),
        out_shape=jax.ShapeDtypeStruct((m, WIDTH_B), BF16),
        scratch_shapes=[
            pltpu.VMEM((seq, 2 * LANES), BF16),
            pltpu.VMEM((2, ACC_ROWS, seq), BF16),
            pltpu.VMEM((2, 2 * LANES, seq), BF16),
            pltpu.VMEM((2, seq, COL_TILE), F32),
            pltpu.VMEM((2, seq, COL_TILE), F32),
            pltpu.VMEM((2, 1, COL_TILE), F32),
            pltpu.VMEM((2, 1, COL_TILE), F32),
        ],
        name="fox_attention",
        compiler_params=pltpu.CompilerParams(
            dimension_semantics=("parallel", "parallel"), vmem_limit_bytes=VMEM_LIMIT),
    )(z, z, z, z, qaug_t, kaug)


def _outproj_kernel(u_ref, v_ref, ga_ref, lng_ref, lnb_ref, ws_ref, bst_ref, yb_ref, x_ref,
                    gate_ref, wa_ref, wb_ref, fg_ref, o_ref, vn_ref, ya_ref, *, final):
    tm, d = x_ref.shape
    n_sub = d // PROJ_SUB
    row_chunks = [slice(r * CHUNK, (r + 1) * CHUNK) for r in range(tm // CHUNK)]
    heads_per_piece = -(-HEADS_A // (n_sub - len(row_chunks)))
    pieces = [functools.partial(_layer_norm_rows, v_ref, lng_ref, lnb_ref, vn_ref, rows)
              for rows in row_chunks]
    for h0 in range(0, HEADS_A, heads_per_piece):
        def gate_heads(h0=h0):
            for h in range(h0, min(h0 + heads_per_piece, HEADS_A)):
                _gate_head(h, u_ref, vn_ref, ga_ref, ws_ref, bst_ref, ya_ref)
        pieces.append(gate_heads)
    assert len(pieces) <= n_sub

    for c in range(n_sub):
        cols = slice(c * PROJ_SUB, (c + 1) * PROJ_SUB)
        y_b = jnp.dot(yb_ref[...], wb_ref[:, cols], preferred_element_type=F32)
        o_ref[:, cols] = x_ref[:, cols] + gate_ref[:, cols] * y_b
        if c < len(pieces):
            pieces[c]()
    x = o_ref[...] + gate_ref[...] * jnp.dot(ya_ref[...], wa_ref[...], preferred_element_type=F32)
    if final:
        x = x * lax.rsqrt(jnp.mean(x * x, axis=-1, keepdims=True) + EPS) * fg_ref[...]
    o_ref[...] = x


def _outproj(z, sgu_params, yb, x2, gate, w_out, final_g, layer, seq, final):
    m, d = x2.shape
    tm = 512
    per_batch = seq // tm
    zcol = lambda col: pl.BlockSpec((tm, WIDTH_A), lambda i: (i, col))
    fixed = lambda shape: pl.BlockSpec(shape, lambda i: (0,) * len(shape))
    return pl.pallas_call(
        functools.partial(_outproj_kernel, final=final),
        grid=(m // tm,),
        in_specs=[
            zcol(COL_U), zcol(COL_V), zcol(COL_GA),
            fixed((1, WIDTH_A)), fixed((1, WIDTH_A)),
            fixed((HEADS_A, CHUNK, CHUNK)), fixed((CHUNK, HEADS_A)),
            pl.BlockSpec((tm, WIDTH_B), lambda i: (i, 0)),
            pl.BlockSpec((tm, d), lambda i: (i, 0)),
            pl.BlockSpec((None, 1, d), lambda i: (i // per_batch, 0, 0)),
            pl.BlockSpec((None, WIDTH_A, d), lambda i: (layer, 0, 0)),
            pl.BlockSpec((None, WIDTH_B, d), lambda i: (layer, 1, 0)),
            fixed((1, d)),
        ],
        out_specs=pl.BlockSpec((tm, d), lambda i: (i, 0)),
        out_shape=jax.ShapeDtypeStruct((m, d), F32),
        scratch_shapes=[pltpu.VMEM((tm, WIDTH_A), BF16), pltpu.VMEM((tm, WIDTH_A), BF16)],
        name="outproj_final" if final else "outproj",
        compiler_params=pltpu.CompilerParams(
            dimension_semantics=("parallel",), vmem_limit_bytes=VMEM_LIMIT),
    )(z, z, z, *sgu_params, yb, x2, gate, w_out, w_out, final_g)


def kernel(x, c, norm_g, w_ada, b_ada, w_in, ln_v_g, ln_v_b, w_s, b_s, b_f, w_out, final_g):
    batch, seq, d = x.shape
    depth = w_in.shape[0]
    c_pad = jnp.pad(c, ((0, 8 - batch), (0, 0)))
    mod = _ada_modulation(c_pad, w_ada, b_ada)[:, :batch]
    mod = mod.reshape(depth, batch, 3, 1, d)
    w_in_bf = w_in.astype(BF16)
    w_out_bf = w_out.astype(BF16)
    x2 = x.reshape(batch * seq, d)
    for l in range(depth):
        shift, scale, gate = mod[l, :, 0], mod[l, :, 1], mod[l, :, 2]
        w_f = jnp.pad(w_in_bf[l, :, N_MAIN:], ((0, 0), (0, LANES - HEADS_B)))
        z, f = _inproj(x2, shift, scale, norm_g[l][None], w_in_bf, w_f, l, seq)
        b_f_pad = jnp.pad(b_f[l], (0, LANES - HEADS_B))[None]
        qaug_t, kaug = _forget_bias(f, b_f_pad, seq)
        yb = _attention(z, qaug_t, kaug, batch, seq)
        sgu_params = (ln_v_g[l][None], ln_v_b[l][None], w_s[l], b_s[l].T)
        x2 = _outproj(z, sgu_params, yb, x2, gate, w_out_bf, final_g[None], l, seq,
                      final=(l == depth - 1))
    return x2.reshape(batch, seq, d)
```

```python
import functools

import jax
import jax.numpy as jnp
import numpy as np
from jax import lax
from jax.experimental import pallas as pl
from jax.experimental.pallas import tpu as pltpu

D_MODEL = 2048
WIDTH_A = 1024
WIDTH_B = 1024
CHUNK = 128
HEADS_A = 8
DH_A = WIDTH_A // HEADS_A
HEADS_B = 16
DH_B = WIDTH_B // HEADS_B
EPS = 1e-6

LANES = 128
HEAD_PAIRS = HEADS_B // 2
BIAS_LANES = 6
N_MAIN = 3 * WIDTH_A + 4 * WIDTH_B
VMEM_LIMIT = 56 * 1024 * 1024

PROJ_SUB = 256
PROJ_TILE = 7 * PROJ_SUB
COL_TILE = 256
KEY_CHUNK = 1024
ONES_ROWS = 16
ACC_ROWS = DH_B + ONES_ROWS

LOG2E = np.float32(1.0 / np.log(2.0))

COL_U, COL_V, COL_GA, COL_Q, COL_K, COL_VB, COL_GB = range(7)

BF16 = jnp.bfloat16
F32 = jnp.float32


def _silu(x):
    return x / (1.0 + jnp.exp(-x))


def _gelu(x):
    return 0.5 * x * (1.0 + lax.erf(x * np.float32(1.0 / np.sqrt(2.0))))


def _ada_kernel(c_ref, w_ref, b_ref, o_ref):
    cond = _silu(c_ref[...]).astype(BF16)
    o_ref[...] = jnp.dot(cond, w_ref[...].astype(BF16), preferred_element_type=F32) + b_ref[...]


def _ada_modulation(c_pad, w_ada, b_ada):
    depth, d, n = w_ada.shape
    tn = 1536
    rows = c_pad.shape[0]
    return pl.pallas_call(
        _ada_kernel,
        grid=(depth, n // tn),
        in_specs=[
            pl.BlockSpec((rows, d), lambda l, j: (0, 0)),
            pl.BlockSpec((None, d, tn), lambda l, j: (l, 0, j)),
            pl.BlockSpec((None, 1, tn), lambda l, j: (l, 0, j)),
        ],
        out_specs=pl.BlockSpec((None, rows, tn), lambda l, j: (l, 0, j)),
        out_shape=jax.ShapeDtypeStruct((depth, rows, n), F32),
        name="ada_modulation",
        compiler_params=pltpu.CompilerParams(
            dimension_semantics=("arbitrary", "arbitrary"), vmem_limit_bytes=VMEM_LIMIT),
    )(c_pad, w_ada, b_ada.reshape(depth, 1, n))


def _inproj_kernel(x_ref, shift_ref, scale_ref, g_ref, w_ref, wf_ref, z_ref, f_ref, h_ref):
    j = pl.program_id(1)

    def normalize():
        x = x_ref[...]
        y = x * lax.rsqrt(jnp.mean(x * x, axis=-1, keepdims=True) + EPS) * g_ref[...]
        h = (y * (1.0 + scale_ref[...]) + shift_ref[...]).astype(BF16)
        h_ref[...] = h
        f_ref[...] = jnp.dot(h, wf_ref[...], preferred_element_type=F32)

    plain = lambda acc: acc
    group_act = (_gelu, _gelu, _silu, lambda acc: acc * (DH_B ** -0.5 * LOG2E), plain, plain, _silu)

    def project(tile_index):
        subs = range(PROJ_TILE // PROJ_SUB)
        acts = [group_act[(tile_index * PROJ_TILE + c * PROJ_SUB) // WIDTH_A] for c in subs]
        for c in sorted(subs, key=lambda c: acts[c] is plain):
            act = acts[c]
            cols = slice(c * PROJ_SUB, (c + 1) * PROJ_SUB)
            acc = jnp.dot(h_ref[...], w_ref[:, cols], preferred_element_type=F32)
            z_ref[:, cols] = act(acc).astype(BF16)

    for tile_index in range(N_MAIN // PROJ_TILE):
        @pl.when(j == tile_index)
        def _(tile_index=tile_index):
            if tile_index == 0:
                normalize()
            project(tile_index)


def _inproj(x2, shift, scale, g, w_in, w_f, layer, seq):
    m, d = x2.shape
    tm, tn = 1024, PROJ_TILE
    per_batch = seq // tm
    return pl.pallas_call(
        _inproj_kernel,
        grid=(m // tm, N_MAIN // tn),
        in_specs=[
            pl.BlockSpec((tm, d), lambda i, j: (i, 0)),
            pl.BlockSpec((None, 1, d), lambda i, j: (i // per_batch, 0, 0)),
            pl.BlockSpec((None, 1, d), lambda i, j: (i // per_batch, 0, 0)),
            pl.BlockSpec((1, d), lambda i, j: (0, 0)),
            pl.BlockSpec((None, d, tn), lambda i, j: (layer, 0, j)),
            pl.BlockSpec((d, LANES), lambda i, j: (0, 0)),
        ],
        out_specs=[
            pl.BlockSpec((tm, tn), lambda i, j: (i, j)),
            pl.BlockSpec((tm, LANES), lambda i, j: (i, 0)),
        ],
        out_shape=[
            jax.ShapeDtypeStruct((m, N_MAIN), BF16),
            jax.ShapeDtypeStruct((m, LANES), F32),
        ],
        scratch_shapes=[pltpu.VMEM((tm, d), BF16)],
        name="norm_inproj",
        compiler_params=pltpu.CompilerParams(
            dimension_semantics=("parallel", "arbitrary"), vmem_limit_bytes=VMEM_LIMIT),
    )(x2, shift, scale, g, w_in, w_f)


def _split3(x):
    hi = x.astype(BF16)
    r1 = x - hi.astype(F32)
    mid = r1.astype(BF16)
    lo = (r1 - mid.astype(F32)).astype(BF16)
    return hi, mid, lo


def _forget_kernel(f_ref, bf_ref, sel_ref, const_ref, qa_ref, ka_ref, carry_ref):
    t = pl.program_id(1)

    @pl.when(t == 0)
    def _():
        carry_ref[...] = jnp.zeros_like(carry_ref)

    x = f_ref[...] + bf_ref[...]
    log_f = -(jnp.maximum(-x, 0.0) + jnp.log1p(jnp.exp(-jnp.abs(x))))
    ts = x.shape[0]
    row = lax.broadcasted_iota(jnp.int32, (ts, ts), 0)
    col = lax.broadcasted_iota(jnp.int32, (ts, ts), 1)
    tri = (row >= col).astype(BF16)
    parts = jnp.dot(tri, jnp.concatenate(_split3(log_f), axis=1), preferred_element_type=F32)
    cum = (parts[:, :LANES] + parts[:, LANES:2 * LANES]) + parts[:, 2 * LANES:] + carry_ref[...]
    carry_ref[...] = cum[ts - 1:ts, :]
    pieces = jnp.concatenate(_split3(cum * LOG2E), axis=1)
    aug = jnp.dot(pieces, sel_ref[...], preferred_element_type=F32) + const_ref[...]
    qa_ref[...] = aug[:, :LANES].T.astype(BF16)
    ka_ref[...] = aug[:, LANES:].astype(BF16)


def _bias_selectors():
    sel = np.zeros((3 * LANES, 2 * LANES), np.float32)
    const = np.zeros((1, 2 * LANES), np.float32)
    for head in range(HEADS_B):
        base = BIAS_LANES * head
        for piece in range(3):
            sel[piece * LANES + head, base + piece] = 1.0
            const[0, base + 3 + piece] = 1.0
            const[0, LANES + base + piece] = 1.0
            sel[piece * LANES + head, LANES + base + 3 + piece] = -1.0
    return jnp.asarray(sel, BF16), jnp.asarray(const)


def _forget_bias(f, b_f_pad, seq):
    m = f.shape[0]
    ts = 512
    per_batch = seq // ts
    sel, const = _bias_selectors()
    fixed = lambda shape: pl.BlockSpec(shape, lambda b, t: (0, 0))
    return pl.pallas_call(
        _forget_kernel,
        grid=(m // seq, per_batch),
        in_specs=[
            pl.BlockSpec((ts, LANES), lambda b, t: (b * per_batch + t, 0)),
            fixed((1, LANES)), fixed(sel.shape), fixed(const.shape),
        ],
        out_specs=[
            pl.BlockSpec((LANES, ts), lambda b, t: (0, b * per_batch + t)),
            pl.BlockSpec((ts, LANES), lambda b, t: (b * per_batch + t, 0)),
        ],
        out_shape=[
            jax.ShapeDtypeStruct((LANES, m), BF16),
            jax.ShapeDtypeStruct((m, LANES), BF16),
        ],
        scratch_shapes=[pltpu.VMEM((1, LANES), F32)],
        name="forget_bias",
        compiler_params=pltpu.CompilerParams(
            dimension_semantics=("parallel", "arbitrary"), vmem_limit_bytes=VMEM_LIMIT),
    )(f, b_f_pad, sel, const)


def _spatial_gating(u_ref, v_ref, ga_ref, lng_ref, lnb_ref, ws_ref, bst_ref, o_ref):
    rows = u_ref.shape[0]
    v = v_ref[...].astype(F32)
    mu = jnp.mean(v, axis=-1, keepdims=True)
    vc = v - mu
    var = jnp.mean(vc * vc, axis=-1, keepdims=True)
    vn = (vc * lax.rsqrt(var + EPS) * lng_ref[...] + lnb_ref[...]).astype(BF16)
    trow = lax.broadcasted_iota(jnp.int32, (CHUNK, CHUNK), 0)
    tcol = lax.broadcasted_iota(jnp.int32, (CHUNK, CHUNK), 1)
    causal = trow >= tcol
    for h in range(HEADS_A):
        w = jnp.where(causal, ws_ref[h], 0.0).astype(BF16)
        bias = bst_ref[:, h:h + 1]
        cols = slice(h * DH_A, (h + 1) * DH_A)
        for c in range(rows // CHUNK):
            rws = slice(c * CHUNK, (c + 1) * CHUNK)
            mixed = jnp.dot(w, vn[rws, cols], preferred_element_type=F32) + bias
            y = u_ref[rws, cols].astype(F32) * mixed * ga_ref[rws, cols].astype(F32)
            o_ref[rws, cols] = y.astype(BF16)


def _attn_kernel(q_ref, k_ref, v_ref, gb_ref, qa_ref, ka_ref, o_ref,
                 kk_ref, vt_ref, qt_ref, s0_ref, s1_ref, mx0_ref, mx1_ref):
    pair = pl.program_id(1)
    seq = q_ref.shape[0]
    n_col = seq // COL_TILE

    def tile(c):
        return slice(c * COL_TILE, (c + 1) * COL_TILE)

    kk_ref[:, :LANES] = k_ref[...]
    kk_ref[:, LANES:] = ka_ref[...]
    ones = jnp.ones((ONES_ROWS, KEY_CHUNK), BF16)
    for j in range(seq // KEY_CHUNK):
        rows = slice(j * KEY_CHUNK, (j + 1) * KEY_CHUNK)
        vt = v_ref[rows, :].T
        vt_ref[0, :, rows] = jnp.concatenate([vt[:DH_B], ones], axis=0)
        vt_ref[1, :, rows] = jnp.concatenate([vt[DH_B:], ones], axis=0)

    def prepare(c):
        q_t = q_ref[tile(c), :].T
        qa_t = qa_ref[:, tile(c)].astype(F32)
        bias_row = lax.broadcasted_iota(jnp.int32, qa_t.shape, 0)
        zeros = jnp.zeros((DH_B, COL_TILE), BF16)
        for h in range(2):
            first = BIAS_LANES * (2 * pair + h)
            own = (bias_row >= first) & (bias_row < first + BIAS_LANES)
            bias_t = jnp.where(own, qa_t, 0.0).astype(BF16)
            q_h = q_t[:DH_B] if h == 0 else q_t[DH_B:]
            head_rows = [q_h, zeros] if h == 0 else [zeros, q_h]
            qt_ref[h, :, tile(c)] = jnp.concatenate(head_rows + [bias_t], axis=0)

    def causal(square):
        key = lax.broadcasted_iota(jnp.int32, square.shape, 0)
        qry = lax.broadcasted_iota(jnp.int32, square.shape, 1)
        return jnp.where(key <= qry, square, -jnp.inf)

    def emit(c, pv):
        o_t = jnp.concatenate([x[:DH_B] / x[DH_B:DH_B + 1] for x in pv], axis=0)
        o_ref[tile(c), :] = (o_t.T * gb_ref[tile(c), :].astype(F32)).astype(BF16)

    def scores(c, buf):
        s_ref, mx_ref = buf
        below = c * COL_TILE
        for h in range(2):
            rhs = qt_ref[h, :, tile(c)]
            square = causal(jnp.dot(kk_ref[tile(c), :], rhs, preferred_element_type=F32))
            s_ref[h, tile(c), :] = square
            mx = jnp.max(square, axis=0, keepdims=True)
            if below:
                full = jnp.dot(kk_ref[:below, :], rhs, preferred_element_type=F32)
                s_ref[h, :below, :] = full
                mx = jnp.maximum(mx, jnp.max(full, axis=0, keepdims=True))
            mx_ref[h] = mx

    origin = pl.multiple_of(jnp.minimum(pl.program_id(0), 0) * COL_TILE, COL_TILE)

    def softmax_pv(c, buf):
        s_ref, mx_ref = buf
        n_keys = (c + 1) * COL_TILE
        pv = []
        for h in range(2):
            p_t = jnp.exp2(s_ref[h, pl.ds(origin, n_keys), :] - mx_ref[h]).astype(BF16)
            pv.append(jnp.dot(vt_ref[h, :, :n_keys], p_t, preferred_element_type=F32))
        emit(c, pv)

    order = tuple(reversed(range(n_col)))
    for c in order:
        prepare(c)

    bufs = ((s0_ref, mx0_ref), (s1_ref, mx1_ref))
    scores(order[0], bufs[0])
    for n, c in enumerate(order):
        if n + 1 < n_col:
            scores(order[n + 1], bufs[(n + 1) % 2])
        softmax_pv(c, bufs[n % 2])


def _attention(z, qaug_t, kaug, batch, seq):
    m = z.shape[0]
    blk = WIDTH_B // LANES
    zblock = lambda col: pl.BlockSpec((seq, LANES), lambda b, p: (b, col * blk + p))
    return pl.pallas_call(
        _attn_kernel,
        grid=(batch, HEAD_PAIRS),
        in_specs=[
            zblock(COL_Q), zblock(COL_K), zblock(COL_VB), zblock(COL_GB),
            pl.BlockSpec((LANES, seq), lambda b, p: (0, b)),
            pl.BlockSpec((seq, LANES), lambda b, p: (b, 0)),
        ],
        out_specs=pl.BlockSpec((seq, LANES), lambda b, p: (b, p)),
        out_shape=jax.ShapeDtypeStruct((m, WIDTH_B), BF16),
        scratch_shapes=[
            pltpu.VMEM((seq, 2 * LANES), BF16),
            pltpu.VMEM((2, ACC_ROWS, seq), BF16),
            pltpu.VMEM((2, 2 * LANES, seq), BF16),
            pltpu.VMEM((2, seq, COL_TILE), F32),
            pltpu.VMEM((2, seq, COL_TILE), F32),
            pltpu.VMEM((2, 1, COL_TILE), F32),
            pltpu.VMEM((2, 1, COL_TILE), F32),
        ],
        name="fox_attention",
        compiler_params=pltpu.CompilerParams(
            dimension_semantics=("parallel", "parallel"), vmem_limit_bytes=VMEM_LIMIT),
    )(z, z, z, z, qaug_t, kaug)


def _outproj_kernel(u_ref, v_ref, ga_ref, lng_ref, lnb_ref, ws_ref, bst_ref, yb_ref, x_ref,
                    gate_ref, wa_ref, wb_ref, fg_ref, o_ref, ya_ref, *, final):
    x = x_ref[...] + gate_ref[...] * jnp.dot(yb_ref[...], wb_ref[...], preferred_element_type=F32)
    _spatial_gating(u_ref, v_ref, ga_ref, lng_ref, lnb_ref, ws_ref, bst_ref, ya_ref)
    x = x + gate_ref[...] * jnp.dot(ya_ref[...], wa_ref[...], preferred_element_type=F32)
    if final:
        x = x * lax.rsqrt(jnp.mean(x * x, axis=-1, keepdims=True) + EPS) * fg_ref[...]
    o_ref[...] = x


def _outproj(z, sgu_params, yb, x2, gate, w_out, final_g, layer, seq, final):
    m, d = x2.shape
    tm = 512
    per_batch = seq // tm
    zcol = lambda col: pl.BlockSpec((tm, WIDTH_A), lambda i: (i, col))
    fixed = lambda shape: pl.BlockSpec(shape, lambda i: (0,) * len(shape))
    return pl.pallas_call(
        functools.partial(_outproj_kernel, final=final),
        grid=(m // tm,),
        in_specs=[
            zcol(COL_U), zcol(COL_V), zcol(COL_GA),
            fixed((1, WIDTH_A)), fixed((1, WIDTH_A)),
            fixed((HEADS_A, CHUNK, CHUNK)), fixed((CHUNK, HEADS_A)),
            pl.BlockSpec((tm, WIDTH_B), lambda i: (i, 0)),
            pl.BlockSpec((tm, d), lambda i: (i, 0)),
            pl.BlockSpec((None, 1, d), lambda i: (i // per_batch, 0, 0)),
            pl.BlockSpec((None, WIDTH_A, d), lambda i: (layer, 0, 0)),
            pl.BlockSpec((None, WIDTH_B, d), lambda i: (layer, 1, 0)),
            fixed((1, d)),
        ],
        out_specs=pl.BlockSpec((tm, d), lambda i: (i, 0)),
        out_shape=jax.ShapeDtypeStruct((m, d), F32),
        scratch_shapes=[pltpu.VMEM((tm, WIDTH_A), BF16)],
        name="outproj_final" if final else "outproj",
        compiler_params=pltpu.CompilerParams(
            dimension_semantics=("parallel",), vmem_limit_bytes=VMEM_LIMIT),
    )(z, z, z, *sgu_params, yb, x2, gate, w_out, w_out, final_g)


def kernel(x, c, norm_g, w_ada, b_ada, w_in, ln_v_g, ln_v_b, w_s, b_s, b_f, w_out, final_g):
    batch, seq, d = x.shape
    depth = w_in.shape[0]
    c_pad = jnp.pad(c, ((0, 8 - batch), (0, 0)))
    mod = _ada_modulation(c_pad, w_ada, b_ada)[:, :batch]
    mod = mod.reshape(depth, batch, 3, 1, d)
    w_in_bf = w_in.astype(BF16)
    w_out_bf = w_out.astype(BF16)
    x2 = x.reshape(batch * seq, d)
    for l in range(depth):
        shift, scale, gate = mod[l, :, 0], mod[l, :, 1], mod[l, :, 2]
        w_f = jnp.pad(w_in_bf[l, :, N_MAIN:], ((0, 0), (0, LANES - HEADS_B)))
        z, f = _inproj(x2, shift, scale, norm_g[l][None], w_in_bf, w_f, l, seq)
        b_f_pad = jnp.pad(b_f[l], (0, LANES - HEADS_B))[None]
        qaug_t, kaug = _forget_bias(f, b_f_pad, seq)
        yb = _attention(z, qaug_t, kaug, batch, seq)
        sgu_params = (ln_v_g[l][None], ln_v_b[l][None], w_s[l], b_s[l].T)
        x2 = _outproj(z, sgu_params, yb, x2, gate, w_out_bf, final_g[None], l, seq,
                      final=(l == depth - 1))
    return x2.reshape(batch, seq, d)
```

```python
import functools

import jax
import jax.numpy as jnp
import numpy as np
from jax import lax
from jax.experimental import pallas as pl
from jax.experimental.pallas import tpu as pltpu

D_MODEL = 2048
WIDTH_A = 1024
WIDTH_B = 1024
CHUNK = 128
HEADS_A = 8
DH_A = WIDTH_A // HEADS_A
HEADS_B = 16
DH_B = WIDTH_B // HEADS_B
EPS = 1e-6

LANES = 128
HEAD_PAIRS = HEADS_B // 2
BIAS_LANES = 6
N_MAIN = 3 * WIDTH_A + 4 * WIDTH_B
VMEM_LIMIT = 56 * 1024 * 1024

PROJ_SUB = 256
PROJ_TILE = 7 * PROJ_SUB
NORM_ROWS = 256
COL_TILE = 256
KEY_CHUNK = 1024
ONES_ROWS = 16
ACC_ROWS = DH_B + ONES_ROWS

LOG2E = np.float32(1.0 / np.log(2.0))

COL_U, COL_V, COL_GA, COL_Q, COL_K, COL_VB, COL_GB = range(7)

BF16 = jnp.bfloat16
F32 = jnp.float32


def _silu(x):
    return x / (1.0 + jnp.exp(-x))


def _gelu(x):
    return 0.5 * x * (1.0 + lax.erf(x * np.float32(1.0 / np.sqrt(2.0))))


def _ada_kernel(c_ref, w_ref, b_ref, o_ref):
    cond = _silu(c_ref[...]).astype(BF16)
    o_ref[...] = jnp.dot(cond, w_ref[...].astype(BF16), preferred_element_type=F32) + b_ref[...]


def _ada_modulation(c_pad, w_ada, b_ada):
    depth, d, n = w_ada.shape
    tn = 1536
    rows = c_pad.shape[0]
    return pl.pallas_call(
        _ada_kernel,
        grid=(depth, n // tn),
        in_specs=[
            pl.BlockSpec((rows, d), lambda l, j: (0, 0)),
            pl.BlockSpec((None, d, tn), lambda l, j: (l, 0, j)),
            pl.BlockSpec((None, 1, tn), lambda l, j: (l, 0, j)),
        ],
        out_specs=pl.BlockSpec((None, rows, tn), lambda l, j: (l, 0, j)),
        out_shape=jax.ShapeDtypeStruct((depth, rows, n), F32),
        name="ada_modulation",
        compiler_params=pltpu.CompilerParams(
            dimension_semantics=("arbitrary", "arbitrary"), vmem_limit_bytes=VMEM_LIMIT),
    )(c_pad, w_ada, b_ada.reshape(depth, 1, n))


def _inproj_kernel(x_ref, shift_ref, scale_ref, g_ref, w_ref, wf_ref, z_ref, f_ref, h_ref):
    j = pl.program_id(1)

    plain = lambda acc: acc
    group_act = (_gelu, _gelu, _silu, lambda acc: acc * (DH_B ** -0.5 * LOG2E), plain, plain, _silu)
    first_cols = slice(0, PROJ_SUB)

    def normalize_and_start():
        for r in range(x_ref.shape[0] // NORM_ROWS):
            rows = slice(r * NORM_ROWS, (r + 1) * NORM_ROWS)
            x = x_ref[rows, :]
            y = x * lax.rsqrt(jnp.mean(x * x, axis=-1, keepdims=True) + EPS) * g_ref[...]
            h = (y * (1.0 + scale_ref[...]) + shift_ref[...]).astype(BF16)
            h_ref[rows, :] = h
            f_ref[rows, :] = jnp.dot(h, wf_ref[...], preferred_element_type=F32)
            acc = jnp.dot(h, w_ref[:, first_cols], preferred_element_type=F32)
            z_ref[rows, first_cols] = group_act[0](acc).astype(BF16)

    def project(tile_index, subs):
        acts = {c: group_act[(tile_index * PROJ_TILE + c * PROJ_SUB) // WIDTH_A] for c in subs}
        for c in sorted(subs, key=lambda c: acts[c] is plain):
            cols = slice(c * PROJ_SUB, (c + 1) * PROJ_SUB)
            acc = jnp.dot(h_ref[...], w_ref[:, cols], preferred_element_type=F32)
            z_ref[:, cols] = acts[c](acc).astype(BF16)

    n_sub = PROJ_TILE // PROJ_SUB
    for tile_index in range(N_MAIN // PROJ_TILE):
        @pl.when(j == tile_index)
        def _(tile_index=tile_index):
            if tile_index == 0:
                normalize_and_start()
                project(tile_index, range(1, n_sub))
            else:
                project(tile_index, range(n_sub))


def _inproj(x2, shift, scale, g, w_in, w_f, layer, seq):
    m, d = x2.shape
    tm, tn = 1024, PROJ_TILE
    per_batch = seq // tm
    return pl.pallas_call(
        _inproj_kernel,
        grid=(m // tm, N_MAIN // tn),
        in_specs=[
            pl.BlockSpec((tm, d), lambda i, j: (i, 0)),
            pl.BlockSpec((None, 1, d), lambda i, j: (i // per_batch, 0, 0)),
            pl.BlockSpec((None, 1, d), lambda i, j: (i // per_batch, 0, 0)),
            pl.BlockSpec((1, d), lambda i, j: (0, 0)),
            pl.BlockSpec((None, d, tn), lambda i, j: (layer, 0, j)),
            pl.BlockSpec((d, LANES), lambda i, j: (0, 0)),
        ],
        out_specs=[
            pl.BlockSpec((tm, tn), lambda i, j: (i, j)),
            pl.BlockSpec((tm, LANES), lambda i, j: (i, 0)),
        ],
        out_shape=[
            jax.ShapeDtypeStruct((m, N_MAIN), BF16),
            jax.ShapeDtypeStruct((m, LANES), F32),
        ],
        scratch_shapes=[pltpu.VMEM((tm, d), BF16)],
        name="norm_inproj",
        compiler_params=pltpu.CompilerParams(
            dimension_semantics=("parallel", "arbitrary"), vmem_limit_bytes=VMEM_LIMIT),
    )(x2, shift, scale, g, w_in, w_f)


def _split3(x):
    hi = x.astype(BF16)
    r1 = x - hi.astype(F32)
    mid = r1.astype(BF16)
    lo = (r1 - mid.astype(F32)).astype(BF16)
    return hi, mid, lo


def _forget_kernel(f_ref, bf_ref, sel_ref, const_ref, qa_ref, ka_ref, carry_ref):
    t = pl.program_id(1)

    @pl.when(t == 0)
    def _():
        carry_ref[...] = jnp.zeros_like(carry_ref)

    x = f_ref[...] + bf_ref[...]
    log_f = -(jnp.maximum(-x, 0.0) + jnp.log1p(jnp.exp(-jnp.abs(x))))
    ts = x.shape[0]
    row = lax.broadcasted_iota(jnp.int32, (ts, ts), 0)
    col = lax.broadcasted_iota(jnp.int32, (ts, ts), 1)
    tri = (row >= col).astype(BF16)
    parts = jnp.dot(tri, jnp.concatenate(_split3(log_f), axis=1), preferred_element_type=F32)
    cum = (parts[:, :LANES] + parts[:, LANES:2 * LANES]) + parts[:, 2 * LANES:] + carry_ref[...]
    carry_ref[...] = cum[ts - 1:ts, :]
    pieces = jnp.concatenate(_split3(cum * LOG2E), axis=1)
    aug = jnp.dot(pieces, sel_ref[...], preferred_element_type=F32) + const_ref[...]
    qa_ref[...] = aug[:, :LANES].T.astype(BF16)
    ka_ref[...] = aug[:, LANES:].astype(BF16)


def _bias_selectors():
    sel = np.zeros((3 * LANES, 2 * LANES), np.float32)
    const = np.zeros((1, 2 * LANES), np.float32)
    for head in range(HEADS_B):
        base = BIAS_LANES * head
        for piece in range(3):
            sel[piece * LANES + head, base + piece] = 1.0
            const[0, base + 3 + piece] = 1.0
            const[0, LANES + base + piece] = 1.0
            sel[piece * LANES + head, LANES + base + 3 + piece] = -1.0
    return jnp.asarray(sel, BF16), jnp.asarray(const)


def _forget_bias(f, b_f_pad, seq):
    m = f.shape[0]
    ts = 512
    per_batch = seq // ts
    sel, const = _bias_selectors()
    fixed = lambda shape: pl.BlockSpec(shape, lambda b, t: (0, 0))
    return pl.pallas_call(
        _forget_kernel,
        grid=(m // seq, per_batch),
        in_specs=[
            pl.BlockSpec((ts, LANES), lambda b, t: (b * per_batch + t, 0)),
            fixed((1, LANES)), fixed(sel.shape), fixed(const.shape),
        ],
        out_specs=[
            pl.BlockSpec((LANES, ts), lambda b, t: (0, b * per_batch + t)),
            pl.BlockSpec((ts, LANES), lambda b, t: (b * per_batch + t, 0)),
        ],
        out_shape=[
            jax.ShapeDtypeStruct((LANES, m), BF16),
            jax.ShapeDtypeStruct((m, LANES), BF16),
        ],
        scratch_shapes=[pltpu.VMEM((1, LANES), F32)],
        name="forget_bias",
        compiler_params=pltpu.CompilerParams(
            dimension_semantics=("parallel", "arbitrary"), vmem_limit_bytes=VMEM_LIMIT),
    )(f, b_f_pad, sel, const)


def _spatial_gating(u_ref, v_ref, ga_ref, lng_ref, lnb_ref, ws_ref, bst_ref, o_ref):
    rows = u_ref.shape[0]
    v = v_ref[...].astype(F32)
    mu = jnp.mean(v, axis=-1, keepdims=True)
    vc = v - mu
    var = jnp.mean(vc * vc, axis=-1, keepdims=True)
    vn = (vc * lax.rsqrt(var + EPS) * lng_ref[...] + lnb_ref[...]).astype(BF16)
    trow = lax.broadcasted_iota(jnp.int32, (CHUNK, CHUNK), 0)
    tcol = lax.broadcasted_iota(jnp.int32, (CHUNK, CHUNK), 1)
    causal = trow >= tcol
    for h in range(HEADS_A):
        w = jnp.where(causal, ws_ref[h], 0.0).astype(BF16)
        bias = bst_ref[:, h:h + 1]
        cols = slice(h * DH_A, (h + 1) * DH_A)
        for c in range(rows // CHUNK):
            rws = slice(c * CHUNK, (c + 1) * CHUNK)
            mixed = jnp.dot(w, vn[rws, cols], preferred_element_type=F32) + bias
            y = u_ref[rws, cols].astype(F32) * mixed * ga_ref[rws, cols].astype(F32)
            o_ref[rws, cols] = y.astype(BF16)


def _attn_kernel(q_ref, k_ref, v_ref, gb_ref, qa_ref, ka_ref, o_ref,
                 kk_ref, vt_ref, qt_ref, s0_ref, s1_ref, mx0_ref, mx1_ref):
    pair = pl.program_id(1)
    seq = q_ref.shape[0]
    n_col = seq // COL_TILE

    def tile(c):
        return slice(c * COL_TILE, (c + 1) * COL_TILE)

    kk_ref[:, :LANES] = k_ref[...]
    kk_ref[:, LANES:] = ka_ref[...]
    ones = jnp.ones((ONES_ROWS, KEY_CHUNK), BF16)
    for j in range(seq // KEY_CHUNK):
        rows = slice(j * KEY_CHUNK, (j + 1) * KEY_CHUNK)
        vt = v_ref[rows, :].T
        vt_ref[0, :, rows] = jnp.concatenate([vt[:DH_B], ones], axis=0)
        vt_ref[1, :, rows] = jnp.concatenate([vt[DH_B:], ones], axis=0)

    def prepare(c):
        q_t = q_ref[tile(c), :].T
        qa_t = qa_ref[:, tile(c)].astype(F32)
        bias_row = lax.broadcasted_iota(jnp.int32, qa_t.shape, 0)
        zeros = jnp.zeros((DH_B, COL_TILE), BF16)
        for h in range(2):
            first = BIAS_LANES * (2 * pair + h)
            own = (bias_row >= first) & (bias_row < first + BIAS_LANES)
            bias_t = jnp.where(own, qa_t, 0.0).astype(BF16)
            q_h = q_t[:DH_B] if h == 0 else q_t[DH_B:]
            head_rows = [q_h, zeros] if h == 0 else [zeros, q_h]
            qt_ref[h, :, tile(c)] = jnp.concatenate(head_rows + [bias_t], axis=0)

    def causal(square):
        key = lax.broadcasted_iota(jnp.int32, square.shape, 0)
        qry = lax.broadcasted_iota(jnp.int32, square.shape, 1)
        return jnp.where(key <= qry, square, -jnp.inf)

    def emit(c, pv):
        o_t = jnp.concatenate([x[:DH_B] / x[DH_B:DH_B + 1] for x in pv], axis=0)
        o_ref[tile(c), :] = (o_t.T * gb_ref[tile(c), :].astype(F32)).astype(BF16)

    def scores(c, buf):
        s_ref, mx_ref = buf
        below = c * COL_TILE
        for h in range(2):
            rhs = qt_ref[h, :, tile(c)]
            square = causal(jnp.dot(kk_ref[tile(c), :], rhs, preferred_element_type=F32))
            s_ref[h, tile(c), :] = square
            mx = jnp.max(square, axis=0, keepdims=True)
            if below:
                full = jnp.dot(kk_ref[:below, :], rhs, preferred_element_type=F32)
                s_ref[h, :below, :] = full
                mx = jnp.maximum(mx, jnp.max(full, axis=0, keepdims=True))
            mx_ref[h] = mx

    origin = pl.multiple_of(jnp.minimum(pl.program_id(0), 0) * COL_TILE, COL_TILE)

    def softmax_pv(c, buf):
        s_ref, mx_ref = buf
        n_keys = (c + 1) * COL_TILE
        pv = []
        for h in range(2):
            p_t = jnp.exp2(s_ref[h, pl.ds(origin, n_keys), :] - mx_ref[h]).astype(BF16)
            pv.append(jnp.dot(vt_ref[h, :, :n_keys], p_t, preferred_element_type=F32))
        emit(c, pv)

    order = tuple(reversed(range(n_col)))
    for c in order:
        prepare(c)

    bufs = ((s0_ref, mx0_ref), (s1_ref, mx1_ref))
    scores(order[0], bufs[0])
    for n, c in enumerate(order):
        if n + 1 < n_col:
            scores(order[n + 1], bufs[(n + 1) % 2])
        softmax_pv(c, bufs[n % 2])


def _attention(z, qaug_t, kaug, batch, seq):
    m = z.shape[0]
    blk = WIDTH_B // LANES
    zblock = lambda col: pl.BlockSpec((seq, LANES), lambda b, p: (b, col * blk + p))
    return pl.pallas_call(
        _attn_kernel,
        grid=(batch, HEAD_PAIRS),
        in_specs=[
            zblock(COL_Q), zblock(COL_K), zblock(COL_VB), zblock(COL_GB),
            pl.BlockSpec((LANES, seq), lambda b, p: (0, b)),
            pl.BlockSpec((seq, LANES), lambda b, p: (b, 0)),
        ],
        out_specs=pl.BlockSpec((seq, LANES), lambda b, p: (b, p)),
        out_shape=jax.ShapeDtypeStruct((m, WIDTH_B), BF16),
        scratch_shapes=[
            pltpu.VMEM((seq, 2 * LANES), BF16),
            pltpu.VMEM((2, ACC_ROWS, seq), BF16),
            pltpu.VMEM((2, 2 * LANES, seq), BF16),
            pltpu.VMEM((2, seq, COL_TILE), F32),
            pltpu.VMEM((2, seq, COL_TILE), F32),
            pltpu.VMEM((2, 1, COL_TILE), F32),
            pltpu.VMEM((2, 1, COL_TILE), F32),
        ],
        name="fox_attention",
        compiler_params=pltpu.CompilerParams(
            dimension_semantics=("parallel", "parallel"), vmem_limit_bytes=VMEM_LIMIT),
    )(z, z, z, z, qaug_t, kaug)


def _outproj_kernel(u_ref, v_ref, ga_ref, lng_ref, lnb_ref, ws_ref, bst_ref, yb_ref, x_ref,
                    gate_ref, wa_ref, wb_ref, fg_ref, o_ref, ya_ref, *, final):
    x = x_ref[...] + gate_ref[...] * jnp.dot(yb_ref[...], wb_ref[...], preferred_element_type=F32)
    _spatial_gating(u_ref, v_ref, ga_ref, lng_ref, lnb_ref, ws_ref, bst_ref, ya_ref)
    x = x + gate_ref[...] * jnp.dot(ya_ref[...], wa_ref[...], preferred_element_type=F32)
    if final:
        x = x * lax.rsqrt(jnp.mean(x * x, axis=-1, keepdims=True) + EPS) * fg_ref[...]
    o_ref[...] = x


def _outproj(z, sgu_params, yb, x2, gate, w_out, final_g, layer, seq, final):
    m, d = x2.shape
    tm = 512
    per_batch = seq // tm
    zcol = lambda col: pl.BlockSpec((tm, WIDTH_A), lambda i: (i, col))
    fixed = lambda shape: pl.BlockSpec(shape, lambda i: (0,) * len(shape))
    return pl.pallas_call(
        functools.partial(_outproj_kernel, final=final),
        grid=(m // tm,),
        in_specs=[
            zcol(COL_U), zcol(COL_V), zcol(COL_GA),
            fixed((1, WIDTH_A)), fixed((1, WIDTH_A)),
            fixed((HEADS_A, CHUNK, CHUNK)), fixed((CHUNK, HEADS_A)),
            pl.BlockSpec((tm, WIDTH_B), lambda i: (i, 0)),
            pl.BlockSpec((tm, d), lambda i: (i, 0)),
            pl.BlockSpec((None, 1, d), lambda i: (i // per_batch, 0, 0)),
            pl.BlockSpec((None, WIDTH_A, d), lambda i: (layer, 0, 0)),
            pl.BlockSpec((None, WIDTH_B, d), lambda i: (layer, 1, 0)),
            fixed((1, d)),
        ],
        out_specs=pl.BlockSpec((tm, d), lambda i: (i, 0)),
        out_shape=jax.ShapeDtypeStruct((m, d), F32),
        scratch_shapes=[pltpu.VMEM((tm, WIDTH_A), BF16)],
        name="outproj_final" if final else "outproj",
        compiler_params=pltpu.CompilerParams(
            dimension_semantics=("parallel",), vmem_limit_bytes=VMEM_LIMIT),
    )(z, z, z, *sgu_params, yb, x2, gate, w_out, w_out, final_g)


def kernel(x, c, norm_g, w_ada, b_ada, w_in, ln_v_g, ln_v_b, w_s, b_s, b_f, w_out, final_g):
    batch, seq, d = x.shape
    depth = w_in.shape[0]
    c_pad = jnp.pad(c, ((0, 8 - batch), (0, 0)))
    mod = _ada_modulation(c_pad, w_ada, b_ada)[:, :batch]
    mod = mod.reshape(depth, batch, 3, 1, d)
    w_in_bf = w_in.astype(BF16)
    w_out_bf = w_out.astype(BF16)
    x2 = x.reshape(batch * seq, d)
    for l in range(depth):
        shift, scale, gate = mod[l, :, 0], mod[l, :, 1], mod[l, :, 2]
        w_f = jnp.pad(w_in_bf[l, :, N_MAIN:], ((0, 0), (0, LANES - HEADS_B)))
        z, f = _inproj(x2, shift, scale, norm_g[l][None], w_in_bf, w_f, l, seq)
        b_f_pad = jnp.pad(b_f[l], (0, LANES - HEADS_B))[None]
        qaug_t, kaug = _forget_bias(f, b_f_pad, seq)
        yb = _attention(z, qaug_t, kaug, batch, seq)
        sgu_params = (ln_v_g[l][None], ln_v_b[l][None], w_s[l], b_s[l].T)
        x2 = _outproj(z, sgu_params, yb, x2, gate, w_out_bf, final_g[None], l, seq,
                      final=(l == depth - 1))
    return x2.reshape(batch, seq, d)
```
